```python
import jax, jax.numpy as jnp
from jax import lax
import numpy as np

D_MODEL = 1024
BATCH = 8
SEQ = 2048
DEPTH = 2
DEC_BATCH = 128
DEC_SEQ = 4
PAST_LEN = 16384
PAGE_SIZE = 128

N_MIXERS = 2
N_LRU_LAYERS = (DEPTH + 1) // 2
N_GM_LAYERS = DEPTH // 2
LRU_WIDTH = D_MODEL
LRU_HEADS = 4
LRU_HEAD_DIM = LRU_WIDTH // LRU_HEADS
CONV_W = 4
LRU_C = 8.0
GM_WIDTH = D_MODEL
GM_GROUPS = 8
GM_GROUP_DIM = GM_WIDTH // GM_GROUPS
CHUNK = 128
D_FF = 3 * D_MODEL
FFN_CONV_W = 3
PLE_DIM = 256
EPS = 1e-6

kernel_name = "hybrid_rglru_gmlp_convffn_decode_step"


def rmsnorm(x, g):
    xf = x.astype(jnp.float32)
    y = xf * lax.rsqrt(jnp.mean(xf * xf, axis=-1, keepdims=True) + EPS)
    return (y * g.astype(jnp.float32)).astype(x.dtype)


def layernorm(x, g, b):
    xf = x.astype(jnp.float32)
    mu = jnp.mean(xf, axis=-1, keepdims=True)
    xc = xf - mu
    y = xc * lax.rsqrt(jnp.mean(xc * xc, axis=-1, keepdims=True) + EPS)
    return (y * g.astype(jnp.float32) + b.astype(jnp.float32)).astype(x.dtype)


def causal_dwconv(x, past, w, b):
    k_w = w.shape[0]
    t = x.shape[1]
    xp = jnp.concatenate([past.astype(x.dtype), x], axis=1)
    y = xp[:, 0:t] * w[0]
    for k in range(1, k_w):
        y = y + xp[:, k:k + t] * w[k]
    return y + b, xp[:, xp.shape[1] - (k_w - 1):]


def block_diag_linear(x, w, b):
    bsz, t, _ = x.shape
    xh = x.reshape(bsz, t, LRU_HEADS, LRU_HEAD_DIM)
    return jnp.einsum('bthi,hij->bthj', xh, w).reshape(bsz, t, LRU_WIDTH) + b


def rg_lru(xc, h0, w_a, b_a, w_x, b_x, lam):
    xf = xc.astype(jnp.float32)
    r = jax.nn.sigmoid(block_diag_linear(xc, w_a, b_a).astype(jnp.float32))
    i = jax.nn.sigmoid(block_diag_linear(xc, w_x, b_x).astype(jnp.float32))
    log_a = -LRU_C * r * jax.nn.softplus(-lam.astype(jnp.float32))
    a = jnp.exp(log_a)
    mult = jnp.sqrt(-jnp.expm1(2.0 * log_a))
    bterm = mult * (i * xf)
    bterm = bterm.at[:, 0].add(a[:, 0] * h0.astype(jnp.float32))

    def combine(c1, c2):
        a1, b1 = c1
        a2, b2 = c2
        return a1 * a2, a2 * b1 + b2

    _, h = lax.associative_scan(combine, (a, bterm), axis=1)
    return h.astype(xc.dtype), h[:, -1].astype(xc.dtype)


def lru_mixer(xn, h0, conv_past, w_in, conv_w, conv_b, w_a, b_a, w_x, b_x, lam, w_out):
    proj = xn @ w_in
    gate, xb = jnp.split(proj, 2, axis=-1)
    xc, conv_tail = causal_dwconv(xb, conv_past, conv_w, conv_b)
    h, h_last = rg_lru(xc, h0, w_a, b_a, w_x, b_x, lam)
    y = (jax.nn.gelu(gate) * h) @ w_out
    return y, h_last, conv_tail


def gmlp_mixer(xn, w_in, ln_g, ln_b, w_s, b_s, w_out):
    z = jax.nn.gelu(xn @ w_in)
    u, v = jnp.split(z, 2, axis=-1)
    v = layernorm(v, ln_g, ln_b)
    bsz, t, _ = v.shape
    n_chunks = -(-t // CHUNK)
    pad = n_chunks * CHUNK - t
    vp = jnp.pad(v, ((0, 0), (0, pad), (0, 0))).reshape(bsz, n_chunks, CHUNK, GM_GROUPS, GM_GROUP_DIM)
    mask = jnp.tril(jnp.ones((CHUNK, CHUNK), dtype=bool))
    ws = jnp.where(mask[None], w_s, jnp.zeros_like(w_s))
    s = jnp.einsum('gts,bcsgd->bctgd', ws, vp) + jnp.transpose(b_s)[None, None, :, :, None]
    s = s.reshape(bsz, n_chunks * CHUNK, GM_WIDTH)[:, :t]
    y = (u * s) @ w_out
    return y, v


def conv_ffn(xn, past, w_up, conv_w, conv_b, w_down):
    up = xn @ w_up
    g, u = jnp.split(up, 2, axis=-1)
    gc, tail = causal_dwconv(g, past, conv_w, conv_b)
    return (jax.nn.gelu(gc) * u) @ w_down, tail


def _trunk(x, p, lru_h_past, lru_conv_past, ffn_conv_past, weights):
    (norm_mix, norm_ffn, norm_ple, norm_final,
     lru_w_in, lru_conv_w, lru_conv_b, lru_w_a, lru_b_a, lru_w_x, lru_b_x, lru_lambda, lru_w_out,
     gm_w_in, gm_ln_g, gm_ln_b, gm_w_s, gm_b_s, gm_w_out,
     ffn_w_up, ffn_conv_w, ffn_conv_b, ffn_w_down,
     ple_w_gate, ple_w_proj) = weights
    h = x
    lru_h_new, lru_conv_new, ffn_conv_new, gm_v_new = [], [], [], []
    for i in range(DEPTH):
        j = i // N_MIXERS
        xn = rmsnorm(h, norm_mix[i])
        if i % N_MIXERS == 0:
            y, h_last, tail = lru_mixer(xn, lru_h_past[j], lru_conv_past[j], lru_w_in[j],
                                        lru_conv_w[j], lru_conv_b[j], lru_w_a[j], lru_b_a[j],
                                        lru_w_x[j], lru_b_x[j], lru_lambda[j], lru_w_out[j])
            lru_h_new.append(h_last)
            lru_conv_new.append(tail)
        else:
            y, v = gmlp_mixer(xn, gm_w_in[j], gm_ln_g[j], gm_ln_b[j], gm_w_s[j], gm_b_s[j], gm_w_out[j])
            gm_v_new.append(v)
        h = h + y
        y, tail = conv_ffn(rmsnorm(h, norm_ffn[i]), ffn_conv_past[i], ffn_w_up[i],
                           ffn_conv_w[i], ffn_conv_b[i], ffn_w_down[i])
        ffn_conv_new.append(tail)
        h = h + y
        gate = jax.nn.sigmoid(rmsnorm(h, norm_ple[i]) @ ple_w_gate[i])
        h = h + gate * (p[i] @ ple_w_proj[i])
    out = rmsnorm(h, norm_final)
    return out, jnp.stack(lru_h_new), jnp.stack(lru_conv_new), jnp.stack(ffn_conv_new), gm_v_new


def setup_inputs(seed: int = 0) -> dict:
    key = jax.random.key(seed)
    ks = iter(jax.random.split(key, 40))
    f32 = jnp.float32

    def nrm(shape, scale):
        return jax.random.normal(next(ks), shape, f32) * scale

    def gain(shape):
        return 1.0 + nrm(shape, 0.02)

    a0 = jax.random.uniform(next(ks), (N_LRU_LAYERS, LRU_WIDTH), f32, 0.9, 0.999)
    return {
        "x_prompt": nrm((BATCH, SEQ, D_MODEL), 1.0),
        "x_sample": nrm((DEC_BATCH, DEC_SEQ, D_MODEL), 1.0),
        "p_prompt": nrm((DEPTH, BATCH, SEQ, PLE_DIM), 1.0),
        "p_sample": nrm((DEPTH, DEC_BATCH, DEC_SEQ, PLE_DIM), 1.0),
        "state_lru_h": nrm((N_LRU_LAYERS, DEC_BATCH, LRU_WIDTH), 0.5),
        "state_lru_conv": nrm((N_LRU_LAYERS, DEC_BATCH, CONV_W - 1, LRU_WIDTH), 1.0),
        "state_ffn_conv": nrm((DEPTH, DEC_BATCH, FFN_CONV_W - 1, D_FF), 1.0),
        "norm_mix": gain((DEPTH, D_MODEL)),
        "norm_ffn": gain((DEPTH, D_MODEL)),
        "norm_ple": gain((DEPTH, D_MODEL)),
        "norm_final": gain((D_MODEL,)),
        "lru_w_in": nrm((N_LRU_LAYERS, D_MODEL, 2 * LRU_WIDTH), D_MODEL ** -0.5),
        "lru_conv_w": nrm((N_LRU_LAYERS, CONV_W, LRU_WIDTH), CONV_W ** -0.5),
        "lru_conv_b": nrm((N_LRU_LAYERS, LRU_WIDTH), 0.01),
        "lru_w_a": nrm((N_LRU_LAYERS, LRU_HEADS, LRU_HEAD_DIM, LRU_HEAD_DIM), LRU_HEAD_DIM ** -0.5),
        "lru_b_a": nrm((N_LRU_LAYERS, LRU_WIDTH), 0.01),
        "lru_w_x": nrm((N_LRU_LAYERS, LRU_HEADS, LRU_HEAD_DIM, LRU_HEAD_DIM), LRU_HEAD_DIM ** -0.5),
        "lru_b_x": nrm((N_LRU_LAYERS, LRU_WIDTH), 0.01),
        "lru_lambda": jnp.log(a0) - jnp.log1p(-a0),
        "lru_w_out": nrm((N_LRU_LAYERS, LRU_WIDTH, D_MODEL), LRU_WIDTH ** -0.5),
        "gm_w_in": nrm((N_GM_LAYERS, D_MODEL, 2 * GM_WIDTH), D_MODEL ** -0.5),
        "gm_ln_g": gain((N_GM_LAYERS, GM_WIDTH)),
        "gm_ln_b": nrm((N_GM_LAYERS, GM_WIDTH), 0.01),
        "gm_w_s": nrm((N_GM_LAYERS, GM_GROUPS, CHUNK, CHUNK), CHUNK ** -0.5),
        "gm_b_s": 1.0 + nrm((N_GM_LAYERS, GM_GROUPS, CHUNK), 0.01),
        "gm_w_out": nrm((N_GM_LAYERS, GM_WIDTH, D_MODEL), GM_WIDTH ** -0.5),
        "ffn_w_up": nrm((DEPTH, D_MODEL, 2 * D_FF), D_MODEL ** -0.5),
        "ffn_conv_w": nrm((DEPTH, FFN_CONV_W, D_FF), FFN_CONV_W ** -0.5),
        "ffn_conv_b": nrm((DEPTH, D_FF), 0.01),
        "ffn_w_down": nrm((DEPTH, D_FF, D_MODEL), D_FF ** -0.5),
        "ple_w_gate": nrm((DEPTH, D_MODEL, D_MODEL), D_MODEL ** -0.5),
        "ple_w_proj": nrm((DEPTH, PLE_DIM, D_MODEL), PLE_DIM ** -0.5),
    }


def reference(x_prompt, x_sample, p_prompt, p_sample, state_lru_h, state_lru_conv, state_ffn_conv,
              norm_mix, norm_ffn, norm_ple, norm_final,
              lru_w_in, lru_conv_w, lru_conv_b, lru_w_a, lru_b_a, lru_w_x, lru_b_x, lru_lambda, lru_w_out,
              gm_w_in, gm_ln_g, gm_ln_b, gm_w_s, gm_b_s, gm_w_out,
              ffn_w_up, ffn_conv_w, ffn_conv_b, ffn_w_down,
              ple_w_gate, ple_w_proj):
    weights = (norm_mix, norm_ffn, norm_ple, norm_final,
               lru_w_in, lru_conv_w, lru_conv_b, lru_w_a, lru_b_a, lru_w_x, lru_b_x, lru_lambda, lru_w_out,
               gm_w_in, gm_ln_g, gm_ln_b, gm_w_s, gm_b_s, gm_w_out,
               ffn_w_up, ffn_conv_w, ffn_conv_b, ffn_w_down,
               ple_w_gate, ple_w_proj)
    dt = x_prompt.dtype
    h0_p = jnp.zeros((N_LRU_LAYERS, BATCH, LRU_WIDTH), dt)
    conv0_p = jnp.zeros((N_LRU_LAYERS, BATCH, CONV_W - 1, LRU_WIDTH), dt)
    ffn0_p = jnp.zeros((DEPTH, BATCH, FFN_CONV_W - 1, D_FF), dt)
    y_prompt, new_lru_h_prompt, new_lru_conv_prompt, new_ffn_conv_prompt, _ = _trunk(
        x_prompt, p_prompt, h0_p, conv0_p, ffn0_p, weights)
    y_sample, new_lru_h_sample, new_lru_conv_sample, new_ffn_conv_sample, gm_v_s = _trunk(
        x_sample, p_sample, state_lru_h, state_lru_conv, state_ffn_conv, weights)
    new_gm_v_sample = jnp.stack(gm_v_s)
    return (y_prompt, y_sample, new_lru_h_prompt, new_lru_conv_prompt, new_ffn_conv_prompt,
            new_lru_h_sample, new_lru_conv_sample, new_ffn_conv_sample, new_gm_v_sample)
```

```python
import functools
import math

import jax
import jax.numpy as jnp
from jax import lax
from jax.experimental import pallas as pl
from jax.experimental.pallas import tpu as pltpu

_EPS = 1e-6
_LRU_C = 8.0
_PAST_LEN = 16384
_SUBLANES = 8
_SEQ_TILE = 512
_FFN_CHUNK = 512
_VMEM_LIMIT = 56 * 1024 * 1024

_F32 = jnp.float32
_BF16 = jnp.bfloat16


def _mm(a, b):
    return jnp.dot(a, b, preferred_element_type=_F32)


def _rmsnorm(x, g):
    ms = jnp.mean(x * x, axis=-1, keepdims=True)
    return (x * lax.rsqrt(ms + _EPS)) * g


def _layernorm(x, g, b):
    mu = jnp.mean(x, axis=-1, keepdims=True)
    xc = x - mu
    y = xc * lax.rsqrt(jnp.mean(xc * xc, axis=-1, keepdims=True) + _EPS)
    return y * g + b


def _gelu(x):
    c = math.sqrt(2.0 / math.pi)
    return x * (0.5 * (1.0 + jnp.tanh(c * (x + 0.044715 * (x * x * x)))))


def _softplus(x):
    return jnp.maximum(x, 0.0) + jnp.log1p(jnp.exp(-jnp.abs(x)))


def _conv_rows(x, prev8, w, b):
    kw = w.shape[0]
    x0 = x[0:_SUBLANES]
    row = lax.broadcasted_iota(jnp.int32, x0.shape, 0)
    y = None
    y0 = None
    for k in range(kw):
        d = kw - 1 - k
        wk = w[k:k + 1]
        if d == 0:
            t, t0 = x * wk, x0 * wk
        else:
            t = pltpu.roll(x, d, 0) * wk
            t0 = jnp.where(row < d, pltpu.roll(prev8, d, 0), pltpu.roll(x0, d, 0)) * wk
        y = t if y is None else y + t
        y0 = t0 if y0 is None else y0 + t0
    return y + b, y0 + b


def _conv_slabs(slabs, w, b, n_out):
    kw = w.shape[0]
    out = []
    for t in range(n_out):
        y = slabs[t] * w[0:1]
        for k in range(1, kw):
            y = y + slabs[t + k] * w[k:k + 1]
        out.append(y + b)
    return out


def _lru_gates(xc, wa_ref, ba_ref, wx_ref, bx_ref, lam_ref, a_s, b_s):
    heads, dh, _ = wa_ref.shape
    sp = _softplus(-lam_ref[...])
    for hh in range(heads):
        sl = slice(hh * dh, (hh + 1) * dh)
        xh = xc[:, sl]
        xhb = xh.astype(_BF16)
        r = jax.nn.sigmoid(_mm(xhb, wa_ref[hh]) + ba_ref[:, sl])
        i = jax.nn.sigmoid(_mm(xhb, wx_ref[hh]) + bx_ref[:, sl])
        log_a = (-_LRU_C * r) * sp[:, sl]
        a = jnp.exp(log_a)
        mult = jnp.sqrt(-jnp.tanh(log_a) * (1.0 + a * a))
        a_s[:, sl] = a
        b_s[:, sl] = mult * (i * xh)


def _scan_rows(a_s, b_s, carry):
    rows, cols = a_s.shape
    row = lax.broadcasted_iota(jnp.int32, (_SUBLANES, cols), 0)

    def body(i, carry):
        r0 = pl.multiple_of(i * _SUBLANES, _SUBLANES)
        a8 = a_s[pl.ds(r0, _SUBLANES), :]
        b8 = b_s[pl.ds(r0, _SUBLANES), :]
        d = 1
        while d < _SUBLANES:
            keep = row >= d
            b8 = jnp.where(keep, a8 * pltpu.roll(b8, d, 0) + b8, b8)
            a8 = jnp.where(keep, a8 * pltpu.roll(a8, d, 0), a8)
            d *= 2
        h8 = a8 * carry + b8
        b_s[pl.ds(r0, _SUBLANES), :] = h8
        return h8[_SUBLANES - 1:_SUBLANES, :]

    return lax.fori_loop(0, rows // _SUBLANES, body, carry)


def _ple(h, p, norm_ref, w_gate_ref, w_proj_ref):
    gate = jax.nn.sigmoid(_mm(_rmsnorm(h, norm_ref[...]).astype(_BF16), w_gate_ref[...]))
    return h + gate * _mm(p.astype(_BF16), w_proj_ref[...])


def _lru_seq_kernel(x_ref, g_ref, w_in_ref, cw_ref, cb_ref, wa_ref, ba_ref, wx_ref, bx_ref,
                    lam_ref, w_out_ref, o_ref, hlast_ref, tail_ref, a_s, b_s, xc_s):
    @pl.when(pl.program_id(1) == 0)
    def _():
        hlast_ref[...] = jnp.zeros_like(hlast_ref)
        tail_ref[...] = jnp.zeros_like(tail_ref)

    x = x_ref[...]
    width = a_s.shape[1]
    rows = x.shape[0]
    proj = _mm(_rmsnorm(x, g_ref[...]).astype(_BF16), w_in_ref[...])
    gate = proj[:, :width]
    xb = proj[:, width:]
    y, y0 = _conv_rows(xb, tail_ref[...], cw_ref[...], cb_ref[...])
    xc_s[...] = y
    xc_s[0:_SUBLANES, :] = y0
    tail_ref[...] = xb[rows - _SUBLANES:rows, :]
    _lru_gates(xc_s[...], wa_ref, ba_ref, wx_ref, bx_ref, lam_ref, a_s, b_s)
    carry = _scan_rows(a_s, b_s, hlast_ref[0:1, :])
    hlast_ref[...] = jnp.broadcast_to(carry, hlast_ref.shape)
    o_ref[...] = x + _mm((_gelu(gate) * b_s[...]).astype(_BF16), w_out_ref[...])


def _gmlp_seq_kernel(x_ref, g_ref, w_in_ref, lng_ref, lnb_ref, ws_ref, bs_ref, w_out_ref,
                     o_ref, s_s):
    x = x_ref[...]
    rows = x.shape[0]
    groups, chunk, _ = ws_ref.shape
    width = s_s.shape[1]
    gd = width // groups
    z = _gelu(_mm(_rmsnorm(x, g_ref[...]).astype(_BF16), w_in_ref[...]))
    u = z[:, :width]
    v = _layernorm(z[:, width:], lng_ref[...], lnb_ref[...]).astype(_BF16)
    tri = (lax.broadcasted_iota(jnp.int32, (chunk, chunk), 0)
           >= lax.broadcasted_iota(jnp.int32, (chunk, chunk), 1))
    for g in range(groups):
        wg = jnp.where(tri, ws_ref[g], 0.0).astype(_BF16)
        cs = slice(g * gd, (g + 1) * gd)
        for c in range(rows // chunk):
            rs = slice(c * chunk, (c + 1) * chunk)
            s_s[rs, cs] = _mm(wg, v[rs, cs]) + bs_ref[:, cs]
    o_ref[...] = x + _mm((u * s_s[...]).astype(_BF16), w_out_ref[...])


def _ffn_seq_kernel(final, x_ref, p_ref, nf_ref, w_up_ref, cw_ref, cb_ref, w_down_ref,
                    np_ref, w_gate_ref, w_proj_ref, nfin_ref, o_ref, tail_ref, xn_s, acc_s):
    @pl.when(pl.program_id(1) == 0)
    def _():
        tail_ref[...] = jnp.zeros_like(tail_ref)

    x = x_ref[...]
    rows = x.shape[0]
    n_chunks = w_down_ref.shape[0]
    xn_s[...] = _rmsnorm(x, nf_ref[...]).astype(_BF16)
    acc_s[...] = jnp.zeros_like(acc_s)

    def chunk(j, carry):
        xn = xn_s[...]
        g = _mm(xn, w_up_ref[j])
        u = _mm(xn, w_up_ref[n_chunks + j])
        y, y0 = _conv_rows(g, tail_ref[j], cw_ref[j], cb_ref[j])
        tail_ref[j] = g[rows - _SUBLANES:rows, :]
        gc = jnp.concatenate([y0, y[_SUBLANES:]], axis=0)
        acc_s[...] += _mm((_gelu(gc) * u).astype(_BF16), w_down_ref[j])
        return carry

    lax.fori_loop(0, n_chunks, chunk, 0)
    h = _ple(x + acc_s[...], p_ref[...], np_ref, w_gate_ref, w_proj_ref)
    o_ref[...] = _rmsnorm(h, nfin_ref[...]) if final else h


def _lru_tm_kernel(n_t, x_ref, h0_ref, past_ref, g_ref, w_in_ref, cw_ref, cb_ref, wa_ref, ba_ref,
                   wx_ref, bx_ref, lam_ref, w_out_ref, o_ref, hlast_ref, tail_ref, a_s, b_s, xc_s):
    x = x_ref[...]
    width = a_s.shape[1]
    bs = x.shape[0] // n_t
    kw = cw_ref.shape[0]
    proj = _mm(_rmsnorm(x, g_ref[...]).astype(_BF16), w_in_ref[...])
    gate = proj[:, :width]
    xb = proj[:, width:]
    slabs = ([past_ref[k * bs:(k + 1) * bs, :] for k in range(kw - 1)]
             + [xb[t * bs:(t + 1) * bs, :] for t in range(n_t)])
    for t, y in enumerate(_conv_slabs(slabs, cw_ref[...], cb_ref[...], n_t)):
        xc_s[t * bs:(t + 1) * bs, :] = y
    for k, s in enumerate(slabs[len(slabs) - (kw - 1):]):
        tail_ref[k * bs:(k + 1) * bs, :] = s
    _lru_gates(xc_s[...], wa_ref, ba_ref, wx_ref, bx_ref, lam_ref, a_s, b_s)
    carry = h0_ref[...]
    for t in range(n_t):
        rs = slice(t * bs, (t + 1) * bs)
        carry = a_s[rs, :] * carry + b_s[rs, :]
        b_s[rs, :] = carry
    hlast_ref[...] = carry
    o_ref[...] = x + _mm((_gelu(gate) * b_s[...]).astype(_BF16), w_out_ref[...])


def _gmlp_tm_kernel(n_t, x_ref, g_ref, w_in_ref, lng_ref, lnb_ref, wsc_ref, bsc_ref, w_out_ref,
                    o_ref, v_ref, s_s):
    x = x_ref[...]
    width = s_s.shape[1]
    bs = x.shape[0] // n_t
    z = _gelu(_mm(_rmsnorm(x, g_ref[...]).astype(_BF16), w_in_ref[...]))
    u = z[:, :width]
    v = _layernorm(z[:, width:], lng_ref[...], lnb_ref[...])
    v_ref[...] = v
    for t in range(n_t):
        s = bsc_ref[t:t + 1, :]
        for q in range(t + 1):
            s = s + wsc_ref[t, q:q + 1, :] * v[q * bs:(q + 1) * bs, :]
        s_s[t * bs:(t + 1) * bs, :] = s
    o_ref[...] = x + _mm((u * s_s[...]).astype(_BF16), w_out_ref[...])


def _ffn_tm_kernel(final, n_t, x_ref, p_ref, past_ref, nf_ref, w_up_ref, cw_ref, cb_ref, w_down_ref,
                   np_ref, w_gate_ref, w_proj_ref, nfin_ref, o_ref, tail_ref):
    x = x_ref[...]
    bs = x.shape[0] // n_t
    n_chunks, cn, _ = w_down_ref.shape
    kw = cw_ref.shape[1]
    xn = _rmsnorm(x, nf_ref[...]).astype(_BF16)
    acc = jnp.zeros_like(x)
    for j in range(n_chunks):
        cs = slice(j * cn, (j + 1) * cn)
        g = _mm(xn, w_up_ref[j])
        u = _mm(xn, w_up_ref[n_chunks + j])
        slabs = ([past_ref[k * bs:(k + 1) * bs, cs] for k in range(kw - 1)]
                 + [g[t * bs:(t + 1) * bs, :] for t in range(n_t)])
        gc = jnp.concatenate(_conv_slabs(slabs, cw_ref[j], cb_ref[j], n_t), axis=0)
        for k, s in enumerate(slabs[len(slabs) - (kw - 1):]):
            tail_ref[k * bs:(k + 1) * bs, cs] = s
        acc = acc + _mm((_gelu(gc) * u).astype(_BF16), w_down_ref[j])
    h = _ple(x + acc, p_ref[...], np_ref, w_gate_ref, w_proj_ref)
    o_ref[...] = _rmsnorm(h, nfin_ref[...]) if final else h


def _const_spec(shape):
    zeros = (0,) * len(shape)
    return pl.BlockSpec(shape, lambda *_: zeros, pipeline_mode=pl.Buffered(1))


def _seq_params():
    return pltpu.CompilerParams(dimension_semantics=("arbitrary", "arbitrary"),
                                vmem_limit_bytes=_VMEM_LIMIT)


def _tm_params():
    return pltpu.CompilerParams(vmem_limit_bytes=_VMEM_LIMIT)


def _row_spec(tm, d):
    return pl.BlockSpec((None, tm, d), lambda b, t: (b, t, 0))


def _state_spec(shape):
    zeros = (0,) * len(shape)
    return pl.BlockSpec((None,) + shape, lambda b, t: (b,) + zeros)


def _lru_seq(x, w):
    bsz, t, d = x.shape
    width = w["w_out"].shape[0]
    tm = min(_SEQ_TILE, t)
    consts = [w["norm"], w["w_in"], w["conv_w"], w["conv_b"], w["w_a"], w["b_a"], w["w_x"], w["b_x"],
              w["lam"], w["w_out"]]
    return pl.pallas_call(
        _lru_seq_kernel,
        grid=(bsz, t // tm),
        in_specs=[_row_spec(tm, d)] + [_const_spec(c.shape) for c in consts],
        out_specs=[_row_spec(tm, d), _state_spec((_SUBLANES, width)), _state_spec((_SUBLANES, width))],
        out_shape=[jax.ShapeDtypeStruct((bsz, t, d), _F32),
                   jax.ShapeDtypeStruct((bsz, _SUBLANES, width), _F32),
                   jax.ShapeDtypeStruct((bsz, _SUBLANES, width), _F32)],
        scratch_shapes=[pltpu.VMEM((tm, width), _F32)] * 3,
        compiler_params=_seq_params(),
        name="lru_seq",
    )(x, *consts)


def _gmlp_seq(x, w):
    bsz, t, d = x.shape
    width = w["w_out"].shape[0]
    tm = min(_SEQ_TILE, t)
    consts = [w["norm"], w["w_in"], w["ln_g"], w["ln_b"], w["w_s"], w["b_s_rows"], w["w_out"]]
    return pl.pallas_call(
        _gmlp_seq_kernel,
        grid=(bsz, t // tm),
        in_specs=[_row_spec(tm, d)] + [_const_spec(c.shape) for c in consts],
        out_specs=_row_spec(tm, d),
        out_shape=jax.ShapeDtypeStruct((bsz, t, d), _F32),
        scratch_shapes=[pltpu.VMEM((tm, width), _F32)],
        compiler_params=_seq_params(),
        name="gmlp_seq",
    )(x, *consts)


def _ffn_seq(x, p_all, layer, w, final):
    bsz, t, d = x.shape
    pdim = p_all.shape[-1]
    n_chunks, cn, _ = w["w_down"].shape
    tm = min(_SEQ_TILE, t)
    consts = [w["norm_ffn"], w["w_up"], w["conv_w"], w["conv_b"], w["w_down"], w["norm_ple"],
              w["w_gate"], w["w_proj"], w["norm_final"]]
    p_spec = pl.BlockSpec((None, None, tm, pdim), lambda b, t: (layer, b, t, 0))
    return pl.pallas_call(
        functools.partial(_ffn_seq_kernel, final),
        grid=(bsz, t // tm),
        in_specs=[_row_spec(tm, d), p_spec] + [_const_spec(c.shape) for c in consts],
        out_specs=[_row_spec(tm, d), _state_spec((n_chunks, _SUBLANES, cn))],
        out_shape=[jax.ShapeDtypeStruct((bsz, t, d), _F32),
                   jax.ShapeDtypeStruct((bsz, n_chunks, _SUBLANES, cn), _F32)],
        scratch_shapes=[pltpu.VMEM((tm, d), _BF16), pltpu.VMEM((tm, d), _F32)],
        compiler_params=_seq_params(),
        name="ffn_seq",
    )(x, p_all, *consts)


def _lru_tm(x2d, h0, past2d, n_t, w):
    rows, d = x2d.shape
    width = w["w_out"].shape[0]
    consts = [w["norm"], w["w_in"], w["conv_w"], w["conv_b"], w["w_a"], w["b_a"], w["w_x"], w["b_x"],
              w["lam"], w["w_out"]]
    return pl.pallas_call(
        functools.partial(_lru_tm_kernel, n_t),
        out_shape=[jax.ShapeDtypeStruct((rows, d), _F32),
                   jax.ShapeDtypeStruct(h0.shape, _F32),
                   jax.ShapeDtypeStruct(past2d.shape, _F32)],
        scratch_shapes=[pltpu.VMEM((rows, width), _F32)] * 3,
        compiler_params=_tm_params(),
        name="lru_tm",
    )(x2d, h0, past2d, *consts)


def _gmlp_tm(x2d, n_t, w):
    rows, d = x2d.shape
    width = w["w_out"].shape[0]
    consts = [w["norm"], w["w_in"], w["ln_g"], w["ln_b"], w["w_s_head"], w["b_s_head"], w["w_out"]]
    return pl.pallas_call(
        functools.partial(_gmlp_tm_kernel, n_t),
        out_shape=[jax.ShapeDtypeStruct((rows, d), _F32), jax.ShapeDtypeStruct((rows, width), _F32)],
        scratch_shapes=[pltpu.VMEM((rows, width), _F32)],
        compiler_params=_tm_params(),
        name="gmlp_tm",
    )(x2d, *consts)


def _ffn_tm(x2d, p2d, past2d, n_t, w, final):
    consts = [w["norm_ffn"], w["w_up"], w["conv_w"], w["conv_b"], w["w_down"], w["norm_ple"],
              w["w_gate"], w["w_proj"], w["norm_final"]]
    return pl.pallas_call(
        functools.partial(_ffn_tm_kernel, final, n_t),
        out_shape=[jax.ShapeDtypeStruct(x2d.shape, _F32), jax.ShapeDtypeStruct(past2d.shape, _F32)],
        compiler_params=_tm_params(),
        name="ffn_tm",
    )(x2d, p2d, past2d, *consts)


def _row(v):
    return v.reshape(1, -1)


def _to_tm(a):
    b, t, c = a.shape
    return jnp.transpose(a, (1, 0, 2)).reshape(t * b, c)


def _from_tm(a2d, n_t):
    rows, c = a2d.shape
    return jnp.transpose(a2d.reshape(n_t, rows // n_t, c), (1, 0, 2))


def kernel(x_prompt, x_sample, p_prompt, p_sample, state_lru_h, state_lru_conv, state_ffn_conv, norm_mix, norm_ffn, norm_ple, norm_final, lru_w_in, lru_conv_w, lru_conv_b, lru_w_a, lru_b_a, lru_w_x, lru_b_x, lru_lambda, lru_w_out, gm_w_in, gm_ln_g, gm_ln_b, gm_w_s, gm_b_s, gm_w_out, ffn_w_up, ffn_conv_w, ffn_conv_b, ffn_w_down, ple_w_gate, ple_w_proj):
    depth = norm_mix.shape[0]
    bsz, seq, d_model = x_prompt.shape
    dec_b, dec_t, _ = x_sample.shape
    d_ff = ffn_w_down.shape[1]
    groups, chunk = gm_w_s.shape[1], gm_w_s.shape[2]
    gd = gm_w_out.shape[1] // groups
    cn = min(_FFN_CHUNK, d_ff)
    n_chunks = d_ff // cn
    tm = min(_SEQ_TILE, seq)
    assert seq % tm == 0 and tm % chunk == 0 and tm % _SUBLANES == 0 and d_ff % cn == 0
    assert _PAST_LEN % chunk == 0 and dec_t <= chunk and dec_b % _SUBLANES == 0

    def lru_weights(i, j):
        return dict(norm=_row(norm_mix[i]), w_in=lru_w_in[j].astype(_BF16), conv_w=lru_conv_w[j],
                    conv_b=_row(lru_conv_b[j]), w_a=lru_w_a[j].astype(_BF16), b_a=_row(lru_b_a[j]),
                    w_x=lru_w_x[j].astype(_BF16), b_x=_row(lru_b_x[j]), lam=_row(lru_lambda[j]),
                    w_out=lru_w_out[j].astype(_BF16))

    def gmlp_weights(i, j):
        head = gm_w_s[j][:, :dec_t, :dec_t]
        return dict(norm=_row(norm_mix[i]), w_in=gm_w_in[j].astype(_BF16), ln_g=_row(gm_ln_g[j]),
                    ln_b=_row(gm_ln_b[j]), w_s=gm_w_s[j],
                    b_s_rows=jnp.repeat(gm_b_s[j].T, gd, axis=1),
                    w_s_head=jnp.repeat(jnp.transpose(head, (1, 2, 0)), gd, axis=2),
                    b_s_head=jnp.repeat(gm_b_s[j][:, :dec_t].T, gd, axis=1),
                    w_out=gm_w_out[j].astype(_BF16))

    def ffn_weights(i):
        w_up = ffn_w_up[i].astype(_BF16).reshape(d_model, 2 * n_chunks, cn)
        kw = ffn_conv_w.shape[1]
        return dict(norm_ffn=_row(norm_ffn[i]), w_up=jnp.transpose(w_up, (1, 0, 2)),
                    conv_w=jnp.transpose(ffn_conv_w[i].reshape(kw, n_chunks, cn), (1, 0, 2)),
                    conv_b=ffn_conv_b[i].reshape(n_chunks, 1, cn),
                    w_down=ffn_w_down[i].astype(_BF16).reshape(n_chunks, cn, d_model),
                    norm_ple=_row(norm_ple[i]), w_gate=ple_w_gate[i].astype(_BF16),
                    w_proj=ple_w_proj[i].astype(_BF16), norm_final=_row(norm_final))

    hp = x_prompt
    hs = _to_tm(x_sample)
    lru_h_p, lru_conv_p, ffn_conv_p = [], [], []
    lru_h_s, lru_conv_s, ffn_conv_s, gm_v_s = [], [], [], []
    for i in range(depth):
        j = i // 2
        if i % 2 == 0:
            w = lru_weights(i, j)
            kw = lru_conv_w.shape[1]
            hp, hlast, tail = _lru_seq(hp, w)
            lru_h_p.append(hlast[:, 0, :])
            lru_conv_p.append(tail[:, _SUBLANES - (kw - 1):, :])
            hs, hlast, tail = _lru_tm(hs, state_lru_h[j], _to_tm(state_lru_conv[j]), dec_t, w)
            lru_h_s.append(hlast)
            lru_conv_s.append(_from_tm(tail, kw - 1))
        else:
            w = gmlp_weights(i, j)
            hp = _gmlp_seq(hp, w)
            hs, v = _gmlp_tm(hs, dec_t, w)
            gm_v_s.append(_from_tm(v, dec_t))
        w = ffn_weights(i)
        kw = ffn_conv_w.shape[1]
        final = i == depth - 1
        hp, tail = _ffn_seq(hp, p_prompt, i, w, final)
        tail = tail[:, :, _SUBLANES - (kw - 1):, :]
        ffn_conv_p.append(jnp.transpose(tail, (0, 2, 1, 3)).reshape(bsz, kw - 1, d_ff))
        hs, tail = _ffn_tm(hs, _to_tm(p_sample[i]), _to_tm(state_ffn_conv[i]), dec_t, w, final)
        ffn_conv_s.append(_from_tm(tail, kw - 1))

    return (hp, _from_tm(hs, dec_t), jnp.stack(lru_h_p), jnp.stack(lru_conv_p), jnp.stack(ffn_conv_p),
            jnp.stack(lru_h_s), jnp.stack(lru_conv_s), jnp.stack(ffn_conv_s), jnp.stack(gm_v_s))
```

```python
import functools
import math

import jax
import jax.numpy as jnp
from jax import lax
from jax.experimental import pallas as pl
from jax.experimental.pallas import tpu as pltpu

_EPS = 1e-6
_LRU_C = 8.0
_PAST_LEN = 16384
_SUBLANES = 8
_SEQ_TILE = 512
_FFN_CHUNK = 1024
_VMEM_LIMIT = 56 * 1024 * 1024

_F32 = jnp.float32
_BF16 = jnp.bfloat16


def _mm(a, b):
    return jnp.dot(a, b, preferred_element_type=_F32)


def _vec(ref, i):
    return ref[i:i + 1, :]


def _rmsnorm(x, g):
    ms = jnp.mean(x * x, axis=-1, keepdims=True)
    return (x * lax.rsqrt(ms + _EPS)) * g


def _layernorm(x, g, b):
    mu = jnp.mean(x, axis=-1, keepdims=True)
    xc = x - mu
    y = xc * lax.rsqrt(jnp.mean(xc * xc, axis=-1, keepdims=True) + _EPS)
    return y * g + b


def _gelu(x):
    c = math.sqrt(2.0 / math.pi)
    return x * (0.5 * (1.0 + jnp.tanh(c * (x + 0.044715 * (x * x * x)))))


def _softplus(x):
    return jnp.maximum(x, 0.0) + jnp.log1p(jnp.exp(-jnp.abs(x)))


def _conv_rows(x, prev8, w, b):
    kw = w.shape[0]
    x0 = x[0:_SUBLANES]
    row = lax.broadcasted_iota(jnp.int32, x0.shape, 0)
    y = None
    y0 = None
    for k in range(kw):
        d = kw - 1 - k
        wk = w[k:k + 1]
        if d == 0:
            t, t0 = x * wk, x0 * wk
        else:
            t = pltpu.roll(x, d, 0) * wk
            t0 = jnp.where(row < d, pltpu.roll(prev8, d, 0), pltpu.roll(x0, d, 0)) * wk
        y = t if y is None else y + t
        y0 = t0 if y0 is None else y0 + t0
    return jnp.concatenate([y0, y[_SUBLANES:]], axis=0) + b


def _conv_slabs(slabs, w, b, n_out):
    kw = w.shape[0]
    out = []
    for t in range(n_out):
        y = slabs[t] * w[0:1]
        for k in range(1, kw):
            y = y + slabs[t + k] * w[k:k + 1]
        out.append(y + b)
    return out


def _lru_gates(xc, wa_ref, ba, wx_ref, bx, lam, a_s, b_s):
    heads, dh, _ = wa_ref.shape
    sp = _softplus(-lam)
    for hh in range(heads):
        sl = slice(hh * dh, (hh + 1) * dh)
        xh = xc[:, sl]
        xhb = xh.astype(_BF16)
        r = jax.nn.sigmoid(_mm(xhb, wa_ref[hh]) + ba[:, sl])
        i = jax.nn.sigmoid(_mm(xhb, wx_ref[hh]) + bx[:, sl])
        log_a = (-_LRU_C * r) * sp[:, sl]
        a = jnp.exp(log_a)
        mult = jnp.sqrt(-jnp.tanh(log_a) * (1.0 + a * a))
        a_s[:, sl] = a
        b_s[:, sl] = mult * (i * xh)


def _scan_rows(a_s, b_s, carry):
    rows, cols = a_s.shape
    row = lax.broadcasted_iota(jnp.int32, (_SUBLANES, cols), 0)

    def body(i, carry):
        r0 = pl.multiple_of(i * _SUBLANES, _SUBLANES)
        a8 = a_s[pl.ds(r0, _SUBLANES), :]
        b8 = b_s[pl.ds(r0, _SUBLANES), :]
        d = 1
        while d < _SUBLANES:
            keep = row >= d
            b8 = jnp.where(keep, a8 * pltpu.roll(b8, d, 0) + b8, b8)
            a8 = jnp.where(keep, a8 * pltpu.roll(a8, d, 0), a8)
            d *= 2
        h8 = a8 * carry + b8
        b_s[pl.ds(r0, _SUBLANES), :] = h8
        return h8[_SUBLANES - 1:_SUBLANES, :]

    return lax.fori_loop(0, rows // _SUBLANES, body, carry)


def _ffn_ple(layer, final, x, p, past_fn, tail_fn, nf_ref, w_up_ref, cw_ref, cb_ref, w_down_ref,
             np_ref, w_gate_ref, w_proj_ref, nfin_ref, acc_s):
    d_ff = w_down_ref.shape[0]
    cn = min(_FFN_CHUNK, d_ff)
    xn = _rmsnorm(x, _vec(nf_ref, layer)).astype(_BF16)
    for j in range(d_ff // cn):
        cs = slice(j * cn, (j + 1) * cn)
        g = _mm(xn, w_up_ref[:, cs])
        u = _mm(xn, w_up_ref[:, d_ff + j * cn:d_ff + (j + 1) * cn])
        gc = past_fn(cs, g, cw_ref[:, cs], cb_ref[layer:layer + 1, cs])
        tail_fn(cs, g)
        d = _mm((_gelu(gc) * u).astype(_BF16), w_down_ref[cs, :])
        if j == 0:
            acc_s[...] = d
        else:
            acc_s[...] += d
    h = x + acc_s[...]
    gate = jax.nn.sigmoid(_mm(_rmsnorm(h, _vec(np_ref, layer)).astype(_BF16), w_gate_ref[...]))
    h = h + gate * _mm(p.astype(_BF16), w_proj_ref[...])
    return _rmsnorm(h, nfin_ref[...]) if final else h


def _lru_seq_kernel(layer, j, x_ref, g_ref, w_in_ref, cw_ref, cb_ref, wa_ref, ba_ref, wx_ref, bx_ref,
                    lam_ref, w_out_ref, o_ref, hlast_ref, tail_ref, a_s, b_s):
    @pl.when(pl.program_id(1) == 0)
    def _():
        hlast_ref[...] = jnp.zeros_like(hlast_ref)
        tail_ref[...] = jnp.zeros_like(tail_ref)

    x = x_ref[...]
    width = a_s.shape[1]
    rows = x.shape[0]
    proj = _mm(_rmsnorm(x, _vec(g_ref, layer)).astype(_BF16), w_in_ref[...])
    gate = proj[:, :width]
    xb = proj[:, width:]
    xc = _conv_rows(xb, tail_ref[...], cw_ref[...], _vec(cb_ref, j))
    tail_ref[...] = xb[rows - _SUBLANES:rows, :]
    _lru_gates(xc, wa_ref, _vec(ba_ref, j), wx_ref, _vec(bx_ref, j), _vec(lam_ref, j), a_s, b_s)
    carry = _scan_rows(a_s, b_s, hlast_ref[0:1, :])
    hlast_ref[...] = jnp.broadcast_to(carry, hlast_ref.shape)
    o_ref[...] = x + _mm((_gelu(gate) * b_s[...]).astype(_BF16), w_out_ref[...])


def _gmlp_seq_kernel(layer, j, x_ref, g_ref, w_in_ref, lng_ref, lnb_ref, ws_ref, bs_ref, w_out_ref,
                     o_ref, s_s):
    x = x_ref[...]
    rows = x.shape[0]
    groups, chunk, _ = ws_ref.shape
    width = s_s.shape[1]
    gd = width // groups
    z = _gelu(_mm(_rmsnorm(x, _vec(g_ref, layer)).astype(_BF16), w_in_ref[...]))
    u = z[:, :width]
    v = _layernorm(z[:, width:], _vec(lng_ref, j), _vec(lnb_ref, j)).astype(_BF16)
    tri = (lax.broadcasted_iota(jnp.int32, (chunk, chunk), 0)
           >= lax.broadcasted_iota(jnp.int32, (chunk, chunk), 1))
    for g in range(groups):
        wg = jnp.where(tri, ws_ref[g], 0.0).astype(_BF16)
        cs = slice(g * gd, (g + 1) * gd)
        for c in range(rows // chunk):
            rs = slice(c * chunk, (c + 1) * chunk)
            s_s[rs, cs] = _mm(wg, v[rs, cs]) + bs_ref[:, cs]
    o_ref[...] = x + _mm((u * s_s[...]).astype(_BF16), w_out_ref[...])


def _ffn_seq_kernel(layer, final, x_ref, p_ref, nf_ref, w_up_ref, cw_ref, cb_ref, w_down_ref,
                    np_ref, w_gate_ref, w_proj_ref, nfin_ref, o_ref, tail_ref, acc_s):
    @pl.when(pl.program_id(1) == 0)
    def _():
        tail_ref[...] = jnp.zeros_like(tail_ref)

    rows = x_ref.shape[0]

    def conv(cs, g, w, b):
        return _conv_rows(g, tail_ref[:, cs], w, b)

    def keep_tail(cs, g):
        tail_ref[:, cs] = g[rows - _SUBLANES:rows, :]

    o_ref[...] = _ffn_ple(layer, final, x_ref[...], p_ref[...], conv, keep_tail, nf_ref, w_up_ref,
                          cw_ref, cb_ref, w_down_ref, np_ref, w_gate_ref, w_proj_ref, nfin_ref, acc_s)


def _lru_tm_kernel(layer, j, n_t, x_ref, h0_ref, past_ref, g_ref, w_in_ref, cw_ref, cb_ref, wa_ref,
                   ba_ref, wx_ref, bx_ref, lam_ref, w_out_ref, o_ref, hlast_ref, tail_ref, a_s, b_s):
    x = x_ref[...]
    width = a_s.shape[1]
    bs = x.shape[0] // n_t
    kw = cw_ref.shape[0]
    proj = _mm(_rmsnorm(x, _vec(g_ref, layer)).astype(_BF16), w_in_ref[...])
    gate = proj[:, :width]
    xb = proj[:, width:]
    slabs = ([past_ref[k * bs:(k + 1) * bs, :] for k in range(kw - 1)]
             + [xb[t * bs:(t + 1) * bs, :] for t in range(n_t)])
    xc = jnp.concatenate(_conv_slabs(slabs, cw_ref[...], _vec(cb_ref, j), n_t), axis=0)
    for k, s in enumerate(slabs[len(slabs) - (kw - 1):]):
        tail_ref[k * bs:(k + 1) * bs, :] = s
    _lru_gates(xc, wa_ref, _vec(ba_ref, j), wx_ref, _vec(bx_ref, j), _vec(lam_ref, j), a_s, b_s)
    carry = h0_ref[...]
    for t in range(n_t):
        rs = slice(t * bs, (t + 1) * bs)
        carry = a_s[rs, :] * carry + b_s[rs, :]
        b_s[rs, :] = carry
    hlast_ref[...] = carry
    o_ref[...] = x + _mm((_gelu(gate) * b_s[...]).astype(_BF16), w_out_ref[...])


def _gmlp_tm_kernel(layer, j, n_t, x_ref, g_ref, w_in_ref, lng_ref, lnb_ref, wsc_ref, bsc_ref, w_out_ref,
                    o_ref, v_ref, s_s):
    x = x_ref[...]
    width = s_s.shape[1]
    bs = x.shape[0] // n_t
    z = _gelu(_mm(_rmsnorm(x, _vec(g_ref, layer)).astype(_BF16), w_in_ref[...]))
    u = z[:, :width]
    v = _layernorm(z[:, width:], _vec(lng_ref, j), _vec(lnb_ref, j))
    v_ref[...] = v
    for t in range(n_t):
        s = bsc_ref[t:t + 1, :]
        for q in range(t + 1):
            s = s + wsc_ref[t, q:q + 1, :] * v[q * bs:(q + 1) * bs, :]
        s_s[t * bs:(t + 1) * bs, :] = s
    o_ref[...] = x + _mm((u * s_s[...]).astype(_BF16), w_out_ref[...])


def _ffn_tm_kernel(layer, final, n_t, x_ref, p_ref, past_ref, nf_ref, w_up_ref, cw_ref, cb_ref, w_down_ref,
                   np_ref, w_gate_ref, w_proj_ref, nfin_ref, o_ref, tail_ref, acc_s):
    bs = x_ref.shape[0] // n_t
    kw = cw_ref.shape[0]
    slabs = {}

    def conv(cs, g, w, b):
        slabs[0] = ([past_ref[k * bs:(k + 1) * bs, cs] for k in range(kw - 1)]
                    + [g[t * bs:(t + 1) * bs, :] for t in range(n_t)])
        return jnp.concatenate(_conv_slabs(slabs[0], w, b, n_t), axis=0)

    def keep_tail(cs, g):
        for k, s in enumerate(slabs[0][len(slabs[0]) - (kw - 1):]):
            tail_ref[k * bs:(k + 1) * bs, cs] = s

    o_ref[...] = _ffn_ple(layer, final, x_ref[...], p_ref[...], conv, keep_tail, nf_ref, w_up_ref,
                          cw_ref, cb_ref, w_down_ref, np_ref, w_gate_ref, w_proj_ref, nfin_ref, acc_s)


def _whole(a):
    zeros = (0,) * a.ndim
    return pl.BlockSpec(a.shape, lambda *_: zeros, pipeline_mode=pl.Buffered(1))


def _layer_of(a, i):
    zeros = (0,) * (a.ndim - 1)
    return pl.BlockSpec((None,) + a.shape[1:], lambda *_: (i,) + zeros, pipeline_mode=pl.Buffered(1))


def _params(n_axes):
    return pltpu.CompilerParams(dimension_semantics=("arbitrary",) * n_axes, vmem_limit_bytes=_VMEM_LIMIT)


def _row_spec(tm, d):
    return pl.BlockSpec((None, tm, d), lambda b, t: (b, t, 0))


def _state_spec(shape):
    zeros = (0,) * len(shape)
    return pl.BlockSpec((None,) + shape, lambda b, t: (b,) + zeros)


def _full_spec(shape):
    zeros = (0,) * len(shape)
    return pl.BlockSpec(shape, lambda *_: zeros)


def _lru_specs(w, j):
    return [_whole(w["norm_mix"]), _layer_of(w["lru_w_in"], j), _layer_of(w["lru_conv_w"], j),
            _whole(w["lru_conv_b"]), _layer_of(w["lru_w_a"], j), _whole(w["lru_b_a"]),
            _layer_of(w["lru_w_x"], j), _whole(w["lru_b_x"]), _whole(w["lru_lambda"]),
            _layer_of(w["lru_w_out"], j)]


def _lru_args(w):
    return [w[k] for k in ("norm_mix", "lru_w_in", "lru_conv_w", "lru_conv_b", "lru_w_a", "lru_b_a",
                           "lru_w_x", "lru_b_x", "lru_lambda", "lru_w_out")]


def _ffn_specs(w, i):
    return [_whole(w["norm_ffn"]), _layer_of(w["ffn_w_up"], i), _layer_of(w["ffn_conv_w"], i),
            _whole(w["ffn_conv_b"]), _layer_of(w["ffn_w_down"], i), _whole(w["norm_ple"]),
            _layer_of(w["ple_w_gate"], i), _layer_of(w["ple_w_proj"], i), _whole(w["norm_final"])]


def _ffn_args(w):
    return [w[k] for k in ("norm_ffn", "ffn_w_up", "ffn_conv_w", "ffn_conv_b", "ffn_w_down", "norm_ple",
                           "ple_w_gate", "ple_w_proj", "norm_final")]


def _lru_seq(x, w, layer, j):
    bsz, t, d = x.shape
    width = w["lru_w_out"].shape[1]
    tm = min(_SEQ_TILE, t)
    return pl.pallas_call(
        functools.partial(_lru_seq_kernel, layer, j),
        grid=(bsz, t // tm),
        in_specs=[_row_spec(tm, d)] + _lru_specs(w, j),
        out_specs=[_row_spec(tm, d), _state_spec((_SUBLANES, width)), _state_spec((_SUBLANES, width))],
        out_shape=[jax.ShapeDtypeStruct((bsz, t, d), _F32),
                   jax.ShapeDtypeStruct((bsz, _SUBLANES, width), _F32),
                   jax.ShapeDtypeStruct((bsz, _SUBLANES, width), _F32)],
        scratch_shapes=[pltpu.VMEM((tm, width), _F32)] * 2,
        compiler_params=_params(2),
        name="lru_seq",
    )(x, *_lru_args(w))


def _gmlp_seq(x, w, layer, j):
    bsz, t, d = x.shape
    width = w["gm_w_out"].shape[1]
    tm = min(_SEQ_TILE, t)
    return pl.pallas_call(
        functools.partial(_gmlp_seq_kernel, layer, j),
        grid=(bsz, t // tm),
        in_specs=[_row_spec(tm, d), _whole(w["norm_mix"]), _layer_of(w["gm_w_in"], j), _whole(w["gm_ln_g"]),
                  _whole(w["gm_ln_b"]), _layer_of(w["gm_w_s"], j), _layer_of(w["gm_b_s_rows"], j),
                  _layer_of(w["gm_w_out"], j)],
        out_specs=_row_spec(tm, d),
        out_shape=jax.ShapeDtypeStruct((bsz, t, d), _F32),
        scratch_shapes=[pltpu.VMEM((tm, width), _F32)],
        compiler_params=_params(2),
        name="gmlp_seq",
    )(x, w["norm_mix"], w["gm_w_in"], w["gm_ln_g"], w["gm_ln_b"], w["gm_w_s"], w["gm_b_s_rows"],
      w["gm_w_out"])


def _ffn_seq(x, p_all, w, layer, final):
    bsz, t, d = x.shape
    pdim = p_all.shape[-1]
    d_ff = w["ffn_w_down"].shape[1]
    tm = min(_SEQ_TILE, t)
    p_spec = pl.BlockSpec((None, None, tm, pdim), lambda b, t: (layer, b, t, 0))
    return pl.pallas_call(
        functools.partial(_ffn_seq_kernel, layer, final),
        grid=(bsz, t // tm),
        in_specs=[_row_spec(tm, d), p_spec] + _ffn_specs(w, layer),
        out_specs=[_row_spec(tm, d), _state_spec((_SUBLANES, d_ff))],
        out_shape=[jax.ShapeDtypeStruct((bsz, t, d), _F32),
                   jax.ShapeDtypeStruct((bsz, _SUBLANES, d_ff), _F32)],
        scratch_shapes=[pltpu.VMEM((tm, d), _F32)],
        compiler_params=_params(2),
        name="ffn_seq",
    )(x, p_all, *_ffn_args(w))


def _lru_tm(x2d, h0, past2d, n_t, w, layer, j):
    rows, d = x2d.shape
    width = w["lru_w_out"].shape[1]
    out_shape = [jax.ShapeDtypeStruct((rows, d), _F32), jax.ShapeDtypeStruct(h0.shape, _F32),
                 jax.ShapeDtypeStruct(past2d.shape, _F32)]
    return pl.pallas_call(
        functools.partial(_lru_tm_kernel, layer, j, n_t),
        grid=(1,),
        in_specs=[_full_spec(x2d.shape), _full_spec(h0.shape), _full_spec(past2d.shape)] + _lru_specs(w, j),
        out_specs=[_full_spec(s.shape) for s in out_shape],
        out_shape=out_shape,
        scratch_shapes=[pltpu.VMEM((rows, width), _F32)] * 2,
        compiler_params=_params(1),
        name="lru_tm",
    )(x2d, h0, past2d, *_lru_args(w))


def _gmlp_tm(x2d, n_t, w, layer, j):
    rows, d = x2d.shape
    width = w["gm_w_out"].shape[1]
    out_shape = [jax.ShapeDtypeStruct((rows, d), _F32), jax.ShapeDtypeStruct((rows, width), _F32)]
    return pl.pallas_call(
        functools.partial(_gmlp_tm_kernel, layer, j, n_t),
        grid=(1,),
        in_specs=[_full_spec(x2d.shape), _whole(w["norm_mix"]), _layer_of(w["gm_w_in"], j), _whole(w["gm_ln_g"]),
                  _whole(w["gm_ln_b"]), _layer_of(w["gm_w_s_head"], j), _layer_of(w["gm_b_s_head"], j),
                  _layer_of(w["gm_w_out"], j)],
        out_specs=[_full_spec(s.shape) for s in out_shape],
        out_shape=out_shape,
        scratch_shapes=[pltpu.VMEM((rows, width), _F32)],
        compiler_params=_params(1),
        name="gmlp_tm",
    )(x2d, w["norm_mix"], w["gm_w_in"], w["gm_ln_g"], w["gm_ln_b"], w["gm_w_s_head"], w["gm_b_s_head"],
      w["gm_w_out"])


def _ffn_tm(x2d, p2d, past2d, n_t, w, layer, final):
    out_shape = [jax.ShapeDtypeStruct(x2d.shape, _F32), jax.ShapeDtypeStruct(past2d.shape, _F32)]
    return pl.pallas_call(
        functools.partial(_ffn_tm_kernel, layer, final, n_t),
        grid=(1,),
        in_specs=[_full_spec(x2d.shape), _full_spec(p2d.shape), _full_spec(past2d.shape)] + _ffn_specs(w, layer),
        out_specs=[_full_spec(s.shape) for s in out_shape],
        out_shape=out_shape,
        scratch_shapes=[pltpu.VMEM(x2d.shape, _F32)],
        compiler_params=_params(1),
        name="ffn_tm",
    )(x2d, p2d, past2d, *_ffn_args(w))


def _to_tm(a):
    b, t, c = a.shape
    return jnp.transpose(a, (1, 0, 2)).reshape(t * b, c)


def _from_tm(a2d, n_t):
    rows, c = a2d.shape
    return jnp.transpose(a2d.reshape(n_t, rows // n_t, c), (1, 0, 2))


def kernel(x_prompt, x_sample, p_prompt, p_sample, state_lru_h, state_lru_conv, state_ffn_conv, norm_mix, norm_ffn, norm_ple, norm_final, lru_w_in, lru_conv_w, lru_conv_b, lru_w_a, lru_b_a, lru_w_x, lru_b_x, lru_lambda, lru_w_out, gm_w_in, gm_ln_g, gm_ln_b, gm_w_s, gm_b_s, gm_w_out, ffn_w_up, ffn_conv_w, ffn_conv_b, ffn_w_down, ple_w_gate, ple_w_proj):
    depth = norm_mix.shape[0]
    bsz, seq, d_model = x_prompt.shape
    dec_b, dec_t, _ = x_sample.shape
    d_ff = ffn_w_down.shape[1]
    groups, chunk = gm_w_s.shape[1], gm_w_s.shape[2]
    gd = gm_w_out.shape[1] // groups
    tm = min(_SEQ_TILE, seq)
    assert seq % tm == 0 and tm % chunk == 0 and tm % _SUBLANES == 0 and d_ff % min(_FFN_CHUNK, d_ff) == 0
    assert _PAST_LEN % chunk == 0 and dec_t <= chunk and dec_b % _SUBLANES == 0

    w = dict(
        norm_mix=norm_mix, norm_ffn=norm_ffn, norm_ple=norm_ple, norm_final=norm_final.reshape(1, -1),
        lru_w_in=lru_w_in.astype(_BF16), lru_conv_w=lru_conv_w, lru_conv_b=lru_conv_b,
        lru_w_a=lru_w_a.astype(_BF16), lru_b_a=lru_b_a, lru_w_x=lru_w_x.astype(_BF16), lru_b_x=lru_b_x,
        lru_lambda=lru_lambda, lru_w_out=lru_w_out.astype(_BF16),
        gm_w_in=gm_w_in.astype(_BF16), gm_ln_g=gm_ln_g, gm_ln_b=gm_ln_b, gm_w_s=gm_w_s,
        gm_w_out=gm_w_out.astype(_BF16),
        gm_b_s_rows=jnp.repeat(jnp.transpose(gm_b_s, (0, 2, 1)), gd, axis=2),
        gm_w_s_head=jnp.repeat(jnp.transpose(gm_w_s[:, :, :dec_t, :dec_t], (0, 2, 3, 1)), gd, axis=3),
        gm_b_s_head=jnp.repeat(jnp.transpose(gm_b_s[:, :, :dec_t], (0, 2, 1)), gd, axis=2),
        ffn_w_up=ffn_w_up.astype(_BF16), ffn_conv_w=ffn_conv_w, ffn_conv_b=ffn_conv_b,
        ffn_w_down=ffn_w_down.astype(_BF16), ple_w_gate=ple_w_gate.astype(_BF16),
        ple_w_proj=ple_w_proj.astype(_BF16),
    )

    hp = x_prompt
    hs = _to_tm(x_sample)
    lru_h_p, lru_conv_p, ffn_conv_p = [], [], []
    lru_h_s, lru_conv_s, ffn_conv_s, gm_v_s = [], [], [], []
    for i in range(depth):
        j = i // 2
        if i % 2 == 0:
            kw = lru_conv_w.shape[1]
            hp, hlast, tail = _lru_seq(hp, w, i, j)
            lru_h_p.append(hlast[:, 0, :])
            lru_conv_p.append(tail[:, _SUBLANES - (kw - 1):, :])
            hs, hlast, tail = _lru_tm(hs, state_lru_h[j], _to_tm(state_lru_conv[j]), dec_t, w, i, j)
            lru_h_s.append(hlast)
            lru_conv_s.append(_from_tm(tail, kw - 1))
        else:
            hp = _gmlp_seq(hp, w, i, j)
            hs, v = _gmlp_tm(hs, dec_t, w, i, j)
            gm_v_s.append(_from_tm(v, dec_t))
        kw = ffn_conv_w.shape[1]
        final = i == depth - 1
        hp, tail = _ffn_seq(hp, p_prompt, w, i, final)
        ffn_conv_p.append(tail[:, _SUBLANES - (kw - 1):, :])
        hs, tail = _ffn_tm(hs, _to_tm(p_sample[i]), _to_tm(state_ffn_conv[i]), dec_t, w, i, final)
        ffn_conv_s.append(_from_tm(tail, kw - 1))

    return (hp, _from_tm(hs, dec_t), jnp.stack(lru_h_p), jnp.stack(lru_conv_p), jnp.stack(ffn_conv_p),
            jnp.stack(lru_h_s), jnp.stack(lru_conv_s), jnp.stack(ffn_conv_s), jnp.stack(gm_v_s))
```

```python
import functools
import math

import jax
import jax.numpy as jnp
from jax import lax
from jax.experimental import pallas as pl
from jax.experimental.pallas import tpu as pltpu

_EPS = 1e-6
_LRU_C = 8.0
_PAST_LEN = 16384
_SUBLANES = 8
_SEQ_TILE = 512
_FFN_TILE = 1024
_FFN_CHUNK = 1024
_VMEM_LIMIT = 56 * 1024 * 1024

_F32 = jnp.float32
_BF16 = jnp.bfloat16


def _mm(a, b):
    return jnp.dot(a, b, preferred_element_type=_F32)


def _vec(ref, i):
    return ref[i:i + 1, :]


def _rmsnorm(x, g):
    ms = jnp.mean(x * x, axis=-1, keepdims=True)
    return (x * lax.rsqrt(ms + _EPS)) * g


def _layernorm(x, g, b):
    mu = jnp.mean(x, axis=-1, keepdims=True)
    xc = x - mu
    y = xc * lax.rsqrt(jnp.mean(xc * xc, axis=-1, keepdims=True) + _EPS)
    return y * g + b


def _gelu(x):
    c = math.sqrt(2.0 / math.pi)
    return x * (0.5 * (1.0 + jnp.tanh(c * (x + 0.044715 * (x * x * x)))))


def _softplus(x):
    return jnp.maximum(x, 0.0) + jnp.log1p(jnp.exp(-jnp.abs(x)))


def _conv_rows(x, prev8, w, b):
    kw = w.shape[0]
    x0 = x[0:_SUBLANES]
    row = lax.broadcasted_iota(jnp.int32, x0.shape, 0)
    y = None
    y0 = None
    for k in range(kw):
        d = kw - 1 - k
        wk = w[k:k + 1]
        if d == 0:
            t, t0 = x * wk, x0 * wk
        else:
            t = pltpu.roll(x, d, 0) * wk
            t0 = jnp.where(row < d, pltpu.roll(prev8, d, 0), pltpu.roll(x0, d, 0)) * wk
        y = t if y is None else y + t
        y0 = t0 if y0 is None else y0 + t0
    return jnp.concatenate([y0, y[_SUBLANES:]], axis=0) + b


def _conv_tm(x, past, w, b):
    kw = w.shape[0]
    rows = x.shape[0]
    rb = past.shape[0] // (kw - 1)
    ext = jnp.concatenate([past, x], axis=0)
    y = ext[0:rows] * w[0:1]
    for k in range(1, kw):
        y = y + ext[k * rb:k * rb + rows] * w[k:k + 1]
    return y + b, ext[rows:]


def _sigmoid(z):
    return 0.5 * jnp.tanh(0.5 * z) + 0.5


def _lru_coeffs(xh, wa, ba, wx, bx, decay):
    xhb = xh.astype(_BF16)
    r = _sigmoid(_mm(xhb, wa) + ba)
    i = _sigmoid(_mm(xhb, wx) + bx)
    m = r * decay
    a = jnp.exp(-m)
    mult = jnp.sqrt(jnp.tanh(m) * (1.0 + a * a))
    return a, mult * (i * xh)


def _scan_slabs(a_s, b_s, carry):
    rb = carry.shape[0]
    n_t = a_s.shape[0] // rb

    def body(t, h):
        rs = pl.ds(pl.multiple_of(t * rb, rb), rb)
        h = a_s[rs, :] * h + b_s[rs, :]
        b_s[rs, :] = h
        return h

    return lax.fori_loop(0, n_t, body, carry, unroll=min(n_t, _SUBLANES))


def _ffn_ple(layer, final, x, p, conv_fn, nf_ref, w_up_ref, cw_ref, cb_ref, w_down_ref,
             np_ref, w_gate_ref, w_proj_ref, nfin_ref, acc_s):
    d_ff = w_down_ref.shape[0]
    cn = min(_FFN_CHUNK, d_ff)
    xn = _rmsnorm(x, _vec(nf_ref, layer)).astype(_BF16)
    for j in range(d_ff // cn):
        cs = slice(j * cn, (j + 1) * cn)
        g = _mm(xn, w_up_ref[:, cs])
        u = _mm(xn, w_up_ref[:, d_ff + j * cn:d_ff + (j + 1) * cn])
        gc = conv_fn(cs, g, cw_ref[:, cs], cb_ref[layer:layer + 1, cs])
        d = _mm((_gelu(gc) * u).astype(_BF16), w_down_ref[cs, :])
        if j == 0:
            acc_s[...] = d
        else:
            acc_s[...] += d
    h = x + acc_s[...]
    gate = _sigmoid(_mm(_rmsnorm(h, _vec(np_ref, layer)).astype(_BF16), w_gate_ref[...]))
    h = h + gate * _mm(p.astype(_BF16), w_proj_ref[...])
    return _rmsnorm(h, nfin_ref[...]) if final else h


def _lru_kernel(layer, j, perm, x_ref, h0_ref, past_ref, *refs):
    if perm:
        p_ref, pt_ref, *refs = refs
    (g_ref, w_in_ref, cw_ref, cb_ref, wa_ref, ba_ref, wx_ref, bx_ref, lam_ref, w_out_ref,
     o_ref, hlast_ref, tail_ref, a_s, b_s) = refs

    @pl.when(pl.program_id(0) == 0)
    def _():
        hlast_ref[...] = h0_ref[...]
        tail_ref[...] = past_ref[...]

    heads, dh, _ = wa_ref.shape
    width = heads * dh
    x = x_ref[...].reshape(-1, x_ref.shape[-1])
    xn = _rmsnorm(x, _vec(g_ref, layer)).astype(_BF16)
    if perm:
        xn = _mm(p_ref[...], xn).astype(_BF16)
    decay = _LRU_C * _softplus(-_vec(lam_ref, j))
    proj = _mm(xn, w_in_ref[...])
    gate = proj[:, :width]
    xc, tail = _conv_tm(proj[:, width:], tail_ref[...], cw_ref[...], _vec(cb_ref, j))
    tail_ref[...] = tail
    for hh in range(heads):
        sl = slice(hh * dh, (hh + 1) * dh)
        a, b = _lru_coeffs(xc[:, sl], wa_ref[hh], ba_ref[j:j + 1, sl], wx_ref[hh], bx_ref[j:j + 1, sl],
                           decay[:, sl])
        a_s[:, sl] = a
        b_s[:, sl] = b
    hlast_ref[...] = _scan_slabs(a_s, b_s, hlast_ref[...])
    y = (_gelu(gate) * b_s[...]).astype(_BF16)
    if perm:
        y = _mm(pt_ref[...], y).astype(_BF16)
    o_ref[...] = (x + _mm(y, w_out_ref[...])).reshape(o_ref.shape)


def _gmlp_seq_kernel(layer, j, x_ref, g_ref, w_in_ref, lng_ref, lnb_ref, ws_ref, bs_ref, w_out_ref,
                     o_ref, s_s):
    x = x_ref[...]
    rows = x.shape[0]
    groups, chunk, _ = ws_ref.shape
    width = s_s.shape[1]
    gd = width // groups
    z = _gelu(_mm(_rmsnorm(x, _vec(g_ref, layer)).astype(_BF16), w_in_ref[...]))
    u = z[:, :width]
    v = _layernorm(z[:, width:], _vec(lng_ref, j), _vec(lnb_ref, j)).astype(_BF16)
    tri = (lax.broadcasted_iota(jnp.int32, (chunk, chunk), 0)
           >= lax.broadcasted_iota(jnp.int32, (chunk, chunk), 1))
    for g in range(groups):
        wg = jnp.where(tri, ws_ref[g], 0.0).astype(_BF16)
        cs = slice(g * gd, (g + 1) * gd)
        for c in range(rows // chunk):
            rs = slice(c * chunk, (c + 1) * chunk)
            s_s[rs, cs] = _mm(wg, v[rs, cs]) + bs_ref[:, cs]
    o_ref[...] = x + _mm((u * s_s[...]).astype(_BF16), w_out_ref[...])


def _ffn_seq_kernel(layer, final, x_ref, p_ref, nf_ref, w_up_ref, cw_ref, cb_ref, w_down_ref,
                    np_ref, w_gate_ref, w_proj_ref, nfin_ref, o_ref, tail_ref, acc_s):
    @pl.when(pl.program_id(1) == 0)
    def _():
        tail_ref[...] = jnp.zeros_like(tail_ref)

    rows = x_ref.shape[0]

    def conv(cs, g, w, b):
        gc = _conv_rows(g, tail_ref[:, cs], w, b)
        tail_ref[:, cs] = g[rows - _SUBLANES:rows, :]
        return gc

    o_ref[...] = _ffn_ple(layer, final, x_ref[...], p_ref[...], conv, nf_ref, w_up_ref,
                          cw_ref, cb_ref, w_down_ref, np_ref, w_gate_ref, w_proj_ref, nfin_ref, acc_s)


def _gmlp_tm_kernel(layer, j, n_t, x_ref, g_ref, w_in_ref, lng_ref, lnb_ref, wsc_ref, bsc_ref, w_out_ref,
                    o_ref, v_ref, s_s):
    x = x_ref[...]
    width = s_s.shape[1]
    bs = x.shape[0] // n_t
    z = _gelu(_mm(_rmsnorm(x, _vec(g_ref, layer)).astype(_BF16), w_in_ref[...]))
    u = z[:, :width]
    v = _layernorm(z[:, width:], _vec(lng_ref, j), _vec(lnb_ref, j))
    v_ref[...] = v
    for t in range(n_t):
        s = bsc_ref[t:t + 1, :]
        for q in range(t + 1):
            s = s + wsc_ref[t, q:q + 1, :] * v[q * bs:(q + 1) * bs, :]
        s_s[t * bs:(t + 1) * bs, :] = s
    o_ref[...] = x + _mm((u * s_s[...]).astype(_BF16), w_out_ref[...])


def _ffn_tm_kernel(layer, final, x_ref, p_ref, past_ref, nf_ref, w_up_ref, cw_ref, cb_ref, w_down_ref,
                   np_ref, w_gate_ref, w_proj_ref, nfin_ref, o_ref, tail_ref, acc_s):
    def conv(cs, g, w, b):
        gc, tail = _conv_tm(g, past_ref[:, cs], w, b)
        tail_ref[:, cs] = tail
        return gc

    o_ref[...] = _ffn_ple(layer, final, x_ref[...], p_ref[...], conv, nf_ref, w_up_ref,
                          cw_ref, cb_ref, w_down_ref, np_ref, w_gate_ref, w_proj_ref, nfin_ref, acc_s)


def _whole(a):
    zeros = (0,) * a.ndim
    return pl.BlockSpec(a.shape, lambda *_: zeros, pipeline_mode=pl.Buffered(1))


def _layer_of(a, i):
    zeros = (0,) * (a.ndim - 1)
    return pl.BlockSpec((None,) + a.shape[1:], lambda *_: (i,) + zeros, pipeline_mode=pl.Buffered(1))


def _params(n_axes):
    return pltpu.CompilerParams(dimension_semantics=("arbitrary",) * n_axes, vmem_limit_bytes=_VMEM_LIMIT)


def _row_spec(tm, d):
    return pl.BlockSpec((None, tm, d), lambda b, t: (b, t, 0))


def _state_spec(shape):
    zeros = (0,) * len(shape)
    return pl.BlockSpec((None,) + shape, lambda b, t: (b,) + zeros)


def _full_spec(shape):
    zeros = (0,) * len(shape)
    return pl.BlockSpec(shape, lambda *_: zeros)


def _lru_specs(w, j):
    return [_whole(w["norm_mix"]), _layer_of(w["lru_w_in"], j), _layer_of(w["lru_conv_w"], j),
            _whole(w["lru_conv_b"]), _layer_of(w["lru_w_a"], j), _whole(w["lru_b_a"]),
            _layer_of(w["lru_w_x"], j), _whole(w["lru_b_x"]), _whole(w["lru_lambda"]),
            _layer_of(w["lru_w_out"], j)]


def _lru_args(w):
    return [w[k] for k in ("norm_mix", "lru_w_in", "lru_conv_w", "lru_conv_b", "lru_w_a", "lru_b_a",
                           "lru_w_x", "lru_b_x", "lru_lambda", "lru_w_out")]


def _ffn_specs(w, i):
    return [_whole(w["norm_ffn"]), _layer_of(w["ffn_w_up"], i), _layer_of(w["ffn_conv_w"], i),
            _whole(w["ffn_conv_b"]), _layer_of(w["ffn_w_down"], i), _whole(w["norm_ple"]),
            _layer_of(w["ple_w_gate"], i), _layer_of(w["ple_w_proj"], i), _whole(w["norm_final"])]


def _ffn_args(w):
    return [w[k] for k in ("norm_ffn", "ffn_w_up", "ffn_conv_w", "ffn_conv_b", "ffn_w_down", "norm_ple",
                           "ple_w_gate", "ple_w_proj", "norm_final")]


def _time_major_perm(bsz, tt):
    r = jnp.arange(bsz * tt)
    src = (r % bsz) * tt + r // bsz
    return (src[:, None] == r[None, :]).astype(_BF16)


def _lru(x, h0, past, w, layer, j):
    width = w["lru_w_out"].shape[1]
    perm = x.ndim == 3
    if perm:
        bsz, t, d = x.shape
        tt = min(_SEQ_TILE // bsz, t)
        assert tt % _SUBLANES == 0 and t % tt == 0 and bsz % _SUBLANES == 0
        rows, grid = bsz * tt, (t // tt,)
        x_spec = pl.BlockSpec((bsz, tt, d), lambda t: (0, t, 0))
        p = _time_major_perm(bsz, tt)
        extra, extra_specs = [p, p.T], [_whole(p), _whole(p)]
    else:
        rows, grid = x.shape[0], (1,)
        x_spec = _full_spec(x.shape)
        extra, extra_specs = [], []
    out_shape = [jax.ShapeDtypeStruct(x.shape, _F32), jax.ShapeDtypeStruct(h0.shape, _F32),
                 jax.ShapeDtypeStruct(past.shape, _F32)]
    return pl.pallas_call(
        functools.partial(_lru_kernel, layer, j, perm),
        grid=grid,
        in_specs=[x_spec, _whole(h0), _whole(past)] + extra_specs + _lru_specs(w, j),
        out_specs=[x_spec, _full_spec(h0.shape), _full_spec(past.shape)],
        out_shape=out_shape,
        scratch_shapes=[pltpu.VMEM((rows, width), _F32)] * 2,
        compiler_params=_params(1),
        name="lru_seq" if perm else "lru_tm",
    )(x, h0, past, *extra, *_lru_args(w))


def _gmlp_seq(x, w, layer, j):
    bsz, t, d = x.shape
    width = w["gm_w_out"].shape[1]
    tm = min(_SEQ_TILE, t)
    return pl.pallas_call(
        functools.partial(_gmlp_seq_kernel, layer, j),
        grid=(bsz, t // tm),
        in_specs=[_row_spec(tm, d), _whole(w["norm_mix"]), _layer_of(w["gm_w_in"], j), _whole(w["gm_ln_g"]),
                  _whole(w["gm_ln_b"]), _layer_of(w["gm_w_s"], j), _layer_of(w["gm_b_s_rows"], j),
                  _layer_of(w["gm_w_out"], j)],
        out_specs=_row_spec(tm, d),
        out_shape=jax.ShapeDtypeStruct((bsz, t, d), _F32),
        scratch_shapes=[pltpu.VMEM((tm, width), _F32)],
        compiler_params=_params(2),
        name="gmlp_seq",
    )(x, w["norm_mix"], w["gm_w_in"], w["gm_ln_g"], w["gm_ln_b"], w["gm_w_s"], w["gm_b_s_rows"],
      w["gm_w_out"])


def _ffn_seq(x, p_all, w, layer, final):
    bsz, t, d = x.shape
    pdim = p_all.shape[-1]
    d_ff = w["ffn_w_down"].shape[1]
    tm = min(_FFN_TILE, t)
    p_spec = pl.BlockSpec((None, None, tm, pdim), lambda b, t: (layer, b, t, 0))
    return pl.pallas_call(
        functools.partial(_ffn_seq_kernel, layer, final),
        grid=(bsz, t // tm),
        in_specs=[_row_spec(tm, d), p_spec] + _ffn_specs(w, layer),
        out_specs=[_row_spec(tm, d), _state_spec((_SUBLANES, d_ff))],
        out_shape=[jax.ShapeDtypeStruct((bsz, t, d), _F32),
                   jax.ShapeDtypeStruct((bsz, _SUBLANES, d_ff), _F32)],
        scratch_shapes=[pltpu.VMEM((tm, d), _F32)],
        compiler_params=_params(2),
        name="ffn_seq",
    )(x, p_all, *_ffn_args(w))


def _gmlp_tm(x2d, n_t, w, layer, j):
    rows, d = x2d.shape
    width = w["gm_w_out"].shape[1]
    out_shape = [jax.ShapeDtypeStruct((rows, d), _F32), jax.ShapeDtypeStruct((rows, width), _F32)]
    return pl.pallas_call(
        functools.partial(_gmlp_tm_kernel, layer, j, n_t),
        grid=(1,),
        in_specs=[_full_spec(x2d.shape), _whole(w["norm_mix"]), _layer_of(w["gm_w_in"], j), _whole(w["gm_ln_g"]),
                  _whole(w["gm_ln_b"]), _layer_of(w["gm_w_s_head"], j), _layer_of(w["gm_b_s_head"], j),
                  _layer_of(w["gm_w_out"], j)],
        out_specs=[_full_spec(s.shape) for s in out_shape],
        out_shape=out_shape,
        scratch_shapes=[pltpu.VMEM((rows, width), _F32)],
        compiler_params=_params(1),
        name="gmlp_tm",
    )(x2d, w["norm_mix"], w["gm_w_in"], w["gm_ln_g"], w["gm_ln_b"], w["gm_w_s_head"], w["gm_b_s_head"],
      w["gm_w_out"])


def _ffn_tm(x2d, p2d, past2d, w, layer, final):
    out_shape = [jax.ShapeDtypeStruct(x2d.shape, _F32), jax.ShapeDtypeStruct(past2d.shape, _F32)]
    return pl.pallas_call(
        functools.partial(_ffn_tm_kernel, layer, final),
        grid=(1,),
        in_specs=[_full_spec(x2d.shape), _full_spec(p2d.shape), _full_spec(past2d.shape)] + _ffn_specs(w, layer),
        out_specs=[_full_spec(s.shape) for s in out_shape],
        out_shape=out_shape,
        scratch_shapes=[pltpu.VMEM(x2d.shape, _F32)],
        compiler_params=_params(1),
        name="ffn_tm",
    )(x2d, p2d, past2d, *_ffn_args(w))


def _to_tm(a):
    b, t, c = a.shape
    return jnp.transpose(a, (1, 0, 2)).reshape(t * b, c)


def _from_tm(a2d, n_t):
    rows, c = a2d.shape
    return jnp.transpose(a2d.reshape(n_t, rows // n_t, c), (1, 0, 2))


def kernel(x_prompt, x_sample, p_prompt, p_sample, state_lru_h, state_lru_conv, state_ffn_conv, norm_mix, norm_ffn, norm_ple, norm_final, lru_w_in, lru_conv_w, lru_conv_b, lru_w_a, lru_b_a, lru_w_x, lru_b_x, lru_lambda, lru_w_out, gm_w_in, gm_ln_g, gm_ln_b, gm_w_s, gm_b_s, gm_w_out, ffn_w_up, ffn_conv_w, ffn_conv_b, ffn_w_down, ple_w_gate, ple_w_proj):
    depth = norm_mix.shape[0]
    bsz, seq, d_model = x_prompt.shape
    dec_b, dec_t, _ = x_sample.shape
    d_ff = ffn_w_down.shape[1]
    groups, chunk = gm_w_s.shape[1], gm_w_s.shape[2]
    gd = gm_w_out.shape[1] // groups
    tm = min(_SEQ_TILE, seq)
    assert seq % tm == 0 and tm % chunk == 0 and tm % _SUBLANES == 0 and d_ff % min(_FFN_CHUNK, d_ff) == 0
    assert _PAST_LEN % chunk == 0 and dec_t <= chunk and dec_b % _SUBLANES == 0
    assert seq % min(_FFN_TILE, seq) == 0

    w = dict(
        norm_mix=norm_mix, norm_ffn=norm_ffn, norm_ple=norm_ple, norm_final=norm_final.reshape(1, -1),
        lru_w_in=lru_w_in.astype(_BF16), lru_conv_w=lru_conv_w, lru_conv_b=lru_conv_b,
        lru_w_a=lru_w_a.astype(_BF16), lru_b_a=lru_b_a, lru_w_x=lru_w_x.astype(_BF16), lru_b_x=lru_b_x,
        lru_lambda=lru_lambda, lru_w_out=lru_w_out.astype(_BF16),
        gm_w_in=gm_w_in.astype(_BF16), gm_ln_g=gm_ln_g, gm_ln_b=gm_ln_b, gm_w_s=gm_w_s,
        gm_w_out=gm_w_out.astype(_BF16),
        gm_b_s_rows=jnp.repeat(jnp.transpose(gm_b_s, (0, 2, 1)), gd, axis=2),
        gm_w_s_head=jnp.repeat(jnp.transpose(gm_w_s[:, :, :dec_t, :dec_t], (0, 2, 3, 1)), gd, axis=3),
        gm_b_s_head=jnp.repeat(jnp.transpose(gm_b_s[:, :, :dec_t], (0, 2, 1)), gd, axis=2),
        ffn_w_up=ffn_w_up.astype(_BF16), ffn_conv_w=ffn_conv_w, ffn_conv_b=ffn_conv_b,
        ffn_w_down=ffn_w_down.astype(_BF16), ple_w_gate=ple_w_gate.astype(_BF16),
        ple_w_proj=ple_w_proj.astype(_BF16),
    )

    hp = x_prompt
    hs = _to_tm(x_sample)
    lru_h_p, lru_conv_p, ffn_conv_p = [], [], []
    lru_h_s, lru_conv_s, ffn_conv_s, gm_v_s = [], [], [], []
    for i in range(depth):
        j = i // 2
        if i % 2 == 0:
            kw = lru_conv_w.shape[1]
            width = lru_w_out.shape[1]
            hp, hlast, tail = _lru(hp, jnp.zeros((bsz, width), _F32), jnp.zeros(((kw - 1) * bsz, width), _F32),
                                   w, i, j)
            lru_h_p.append(hlast)
            lru_conv_p.append(_from_tm(tail, kw - 1))
            hs, hlast, tail = _lru(hs, state_lru_h[j], _to_tm(state_lru_conv[j]), w, i, j)
            lru_h_s.append(hlast)
            lru_conv_s.append(_from_tm(tail, kw - 1))
        else:
            hp = _gmlp_seq(hp, w, i, j)
            hs, v = _gmlp_tm(hs, dec_t, w, i, j)
            gm_v_s.append(_from_tm(v, dec_t))
        kw = ffn_conv_w.shape[1]
        final = i == depth - 1
        hp, tail = _ffn_seq(hp, p_prompt, w, i, final)
        ffn_conv_p.append(tail[:, _SUBLANES - (kw - 1):, :])
        hs, tail = _ffn_tm(hs, _to_tm(p_sample[i]), _to_tm(state_ffn_conv[i]), w, i, final)
        ffn_conv_s.append(_from_tm(tail, kw - 1))

    return (hp, _from_tm(hs, dec_t), jnp.stack(lru_h_p), jnp.stack(lru_conv_p), jnp.stack(ffn_conv_p),
            jnp.stack(lru_h_s), jnp.stack(lru_conv_s), jnp.stack(ffn_conv_s), jnp.stack(gm_v_s))
```

```python
import functools
import math

import jax
import jax.numpy as jnp
from jax import lax
from jax.experimental import pallas as pl
from jax.experimental.pallas import tpu as pltpu

_EPS = 1e-6
_LRU_C = 8.0
_PAST_LEN = 16384
_SUBLANES = 8
_SEQ_TILE = 512
_FFN_TILE = 1024
_FFN_CHUNK = 1024
_PLE_ROWS = 256
_VMEM_LIMIT = 56 * 1024 * 1024

_F32 = jnp.float32
_BF16 = jnp.bfloat16


def _mm(a, b):
    return jnp.dot(a, b, preferred_element_type=_F32)


def _vec(ref, i):
    return ref[i:i + 1, :]


def _rmsnorm(x, g):
    ms = jnp.mean(x * x, axis=-1, keepdims=True)
    return (x * lax.rsqrt(ms + _EPS)) * g


def _layernorm(x, g, b):
    mu = jnp.mean(x, axis=-1, keepdims=True)
    xc = x - mu
    y = xc * lax.rsqrt(jnp.mean(xc * xc, axis=-1, keepdims=True) + _EPS)
    return y * g + b


def _gelu(x):
    c = math.sqrt(2.0 / math.pi)
    return x * (0.5 * (1.0 + jnp.tanh(c * (x + 0.044715 * (x * x * x)))))


def _softplus(x):
    return jnp.maximum(x, 0.0) + jnp.log1p(jnp.exp(-jnp.abs(x)))


def _conv_rows(x, prev8, w, b):
    kw = w.shape[0]
    x0 = x[0:_SUBLANES]
    row = lax.broadcasted_iota(jnp.int32, x0.shape, 0)
    y = None
    y0 = None
    for k in range(kw):
        d = kw - 1 - k
        wk = w[k:k + 1]
        if d == 0:
            t, t0 = x * wk, x0 * wk
        else:
            t = pltpu.roll(x, d, 0) * wk
            t0 = jnp.where(row < d, pltpu.roll(prev8, d, 0), pltpu.roll(x0, d, 0)) * wk
        y = t if y is None else y + t
        y0 = t0 if y0 is None else y0 + t0
    return jnp.concatenate([y0, y[_SUBLANES:]], axis=0) + b


def _conv_tm(x, past, w, b):
    kw = w.shape[0]
    rows = x.shape[0]
    rb = past.shape[0] // (kw - 1)
    ext = jnp.concatenate([past, x], axis=0)
    y = ext[0:rows] * w[0:1]
    for k in range(1, kw):
        y = y + ext[k * rb:k * rb + rows] * w[k:k + 1]
    return y + b, ext[rows:]


def _sigmoid(z):
    return 0.5 * jnp.tanh(0.5 * z) + 0.5


def _lru_coeffs(xh, wa, ba, wx, bx, decay):
    xhb = xh.astype(_BF16)
    r = _sigmoid(_mm(xhb, wa) + ba)
    i = _sigmoid(_mm(xhb, wx) + bx)
    m = r * decay
    a = jnp.exp(-m)
    mult = jnp.sqrt(jnp.tanh(m) * (1.0 + a * a))
    return a, mult * (i * xh)


def _scan_slabs(a_s, b_s, carry):
    rb = carry.shape[0]
    n_t = a_s.shape[0] // rb

    def body(t, h):
        rs = pl.ds(pl.multiple_of(t * rb, rb), rb)
        h = a_s[rs, :] * h + b_s[rs, :]
        b_s[rs, :] = h
        return h

    return lax.fori_loop(0, n_t, body, carry, unroll=min(n_t, _SUBLANES))


def _ffn_ple(layer, final, x_ref, p_ref, conv_fn, nf_ref, w_up_ref, cw_ref, cb_ref, w_down_ref,
             np_ref, w_gate_ref, w_proj_ref, nfin_ref, o_ref, acc_s):
    rows = x_ref.shape[0]
    d_ff = w_down_ref.shape[0]
    cn = min(_FFN_CHUNK, d_ff)
    rbk = min(_PLE_ROWS, rows)
    blocks = [slice(r * rbk, (r + 1) * rbk) for r in range(rows // rbk)]
    xn_blocks = [_rmsnorm(x_ref[rs, :], _vec(nf_ref, layer)).astype(_BF16) for rs in blocks]
    xn = jnp.concatenate(xn_blocks, axis=0)
    for j in range(d_ff // cn):
        cs = slice(j * cn, (j + 1) * cn)
        us = slice(d_ff + j * cn, d_ff + (j + 1) * cn)
        if j == 0:
            g = jnp.concatenate([_mm(xb, w_up_ref[:, cs]) for xb in xn_blocks], axis=0)
            u = jnp.concatenate([_mm(xb, w_up_ref[:, us]) for xb in xn_blocks], axis=0)
        else:
            g = _mm(xn, w_up_ref[:, cs])
            u = _mm(xn, w_up_ref[:, us])
        gc = conv_fn(cs, g, cw_ref[:, cs], cb_ref[layer:layer + 1, cs])
        d = _mm((_gelu(gc) * u).astype(_BF16), w_down_ref[cs, :])
        if j == 0:
            acc_s[...] = d
        else:
            acc_s[...] += d
    for rs in blocks:
        h = x_ref[rs, :] + acc_s[rs, :]
        gate = _sigmoid(_mm(_rmsnorm(h, _vec(np_ref, layer)).astype(_BF16), w_gate_ref[...]))
        h = h + gate * _mm(p_ref[rs, :].astype(_BF16), w_proj_ref[...])
        o_ref[rs, :] = _rmsnorm(h, nfin_ref[...]) if final else h


def _lru_kernel(layer, j, perm, x_ref, h0_ref, past_ref, *refs):
    if perm:
        p_ref, pt_ref, *refs = refs
    (g_ref, w_in_ref, cw_ref, cb_ref, wa_ref, ba_ref, wx_ref, bx_ref, lam_ref, w_out_ref,
     o_ref, hlast_ref, tail_ref, a_s, b_s) = refs

    @pl.when(pl.program_id(0) == 0)
    def _():
        hlast_ref[...] = h0_ref[...]
        tail_ref[...] = past_ref[...]

    heads, dh, _ = wa_ref.shape
    width = heads * dh
    x = x_ref[...].reshape(-1, x_ref.shape[-1])
    xn = _rmsnorm(x, _vec(g_ref, layer)).astype(_BF16)
    if perm:
        xn = _mm(p_ref[...], xn).astype(_BF16)
    decay = _LRU_C * _softplus(-_vec(lam_ref, j))
    xb = _mm(xn, w_in_ref[:, width:])
    gate = _mm(xn, w_in_ref[:, :width])
    xc, tail = _conv_tm(xb, tail_ref[...], cw_ref[...], _vec(cb_ref, j))
    tail_ref[...] = tail
    for hh in range(heads):
        sl = slice(hh * dh, (hh + 1) * dh)
        a, b = _lru_coeffs(xc[:, sl], wa_ref[hh], ba_ref[j:j + 1, sl], wx_ref[hh], bx_ref[j:j + 1, sl],
                           decay[:, sl])
        a_s[:, sl] = a
        b_s[:, sl] = b
    hlast_ref[...] = _scan_slabs(a_s, b_s, hlast_ref[...])
    y = (_gelu(gate) * b_s[...]).astype(_BF16)
    if perm:
        y = _mm(pt_ref[...], y).astype(_BF16)
    o_ref[...] = (x + _mm(y, w_out_ref[...])).reshape(o_ref.shape)


def _gmlp_seq_kernel(layer, j, x_ref, g_ref, w_in_ref, lng_ref, lnb_ref, ws_ref, bs_ref, w_out_ref,
                     o_ref, s_s):
    x = x_ref[...]
    rows = x.shape[0]
    groups, chunk, _ = ws_ref.shape
    width = s_s.shape[1]
    gd = width // groups
    xn = _rmsnorm(x, _vec(g_ref, layer)).astype(_BF16)
    n_piece = 2
    pw = width // n_piece
    zv = [_mm(xn, w_in_ref[:, width + k * pw:width + (k + 1) * pw]) for k in range(n_piece)]
    zu = [_mm(xn, w_in_ref[:, k * pw:(k + 1) * pw]) for k in range(n_piece)]
    v = jnp.concatenate([_gelu(z) for z in zv], axis=1)
    v = _layernorm(v, _vec(lng_ref, j), _vec(lnb_ref, j)).astype(_BF16)
    tri = (lax.broadcasted_iota(jnp.int32, (chunk, chunk), 0)
           >= lax.broadcasted_iota(jnp.int32, (chunk, chunk), 1))
    n_c = rows // chunk
    for g in range(groups):
        wg = jnp.where(tri, ws_ref[g], 0.0).astype(_BF16)
        cs = slice(g * gd, (g + 1) * gd)
        vg = jnp.concatenate([v[c * chunk:(c + 1) * chunk, cs] for c in range(n_c)], axis=1)
        sg = _mm(wg, vg)
        for c in range(n_c):
            s_s[c * chunk:(c + 1) * chunk, cs] = sg[:, c * gd:(c + 1) * gd] + bs_ref[:, cs]
    u = jnp.concatenate([_gelu(z) for z in zu], axis=1)
    o_ref[...] = x + _mm((u * s_s[...]).astype(_BF16), w_out_ref[...])


def _ffn_seq_kernel(layer, final, x_ref, p_ref, nf_ref, w_up_ref, cw_ref, cb_ref, w_down_ref,
                    np_ref, w_gate_ref, w_proj_ref, nfin_ref, o_ref, tail_ref, acc_s):
    @pl.when(pl.program_id(1) == 0)
    def _():
        tail_ref[...] = jnp.zeros_like(tail_ref)

    rows = x_ref.shape[0]

    def conv(cs, g, w, b):
        gc = _conv_rows(g, tail_ref[:, cs], w, b)
        tail_ref[:, cs] = g[rows - _SUBLANES:rows, :]
        return gc

    _ffn_ple(layer, final, x_ref, p_ref, conv, nf_ref, w_up_ref, cw_ref, cb_ref, w_down_ref,
             np_ref, w_gate_ref, w_proj_ref, nfin_ref, o_ref, acc_s)


def _gmlp_tm_kernel(layer, j, n_t, x_ref, g_ref, w_in_ref, lng_ref, lnb_ref, wsc_ref, bsc_ref, w_out_ref,
                    o_ref, v_ref, s_s):
    x = x_ref[...]
    width = s_s.shape[1]
    bs = x.shape[0] // n_t
    z = _gelu(_mm(_rmsnorm(x, _vec(g_ref, layer)).astype(_BF16), w_in_ref[...]))
    u = z[:, :width]
    v = _layernorm(z[:, width:], _vec(lng_ref, j), _vec(lnb_ref, j))
    v_ref[...] = v
    for t in range(n_t):
        s = bsc_ref[t:t + 1, :]
        for q in range(t + 1):
            s = s + wsc_ref[t, q:q + 1, :] * v[q * bs:(q + 1) * bs, :]
        s_s[t * bs:(t + 1) * bs, :] = s
    o_ref[...] = x + _mm((u * s_s[...]).astype(_BF16), w_out_ref[...])


def _ffn_tm_kernel(layer, final, x_ref, p_ref, past_ref, nf_ref, w_up_ref, cw_ref, cb_ref, w_down_ref,
                   np_ref, w_gate_ref, w_proj_ref, nfin_ref, o_ref, tail_ref, acc_s):
    def conv(cs, g, w, b):
        gc, tail = _conv_tm(g, past_ref[:, cs], w, b)
        tail_ref[:, cs] = tail
        return gc

    _ffn_ple(layer, final, x_ref, p_ref, conv, nf_ref, w_up_ref, cw_ref, cb_ref, w_down_ref,
             np_ref, w_gate_ref, w_proj_ref, nfin_ref, o_ref, acc_s)


def _whole(a):
    zeros = (0,) * a.ndim
    return pl.BlockSpec(a.shape, lambda *_: zeros, pipeline_mode=pl.Buffered(1))


def _layer_of(a, i):
    zeros = (0,) * (a.ndim - 1)
    return pl.BlockSpec((None,) + a.shape[1:], lambda *_: (i,) + zeros, pipeline_mode=pl.Buffered(1))


def _params(n_axes):
    return pltpu.CompilerParams(dimension_semantics=("arbitrary",) * n_axes, vmem_limit_bytes=_VMEM_LIMIT)


def _row_spec(tm, d):
    return pl.BlockSpec((None, tm, d), lambda b, t: (b, t, 0))


def _state_spec(shape):
    zeros = (0,) * len(shape)
    return pl.BlockSpec((None,) + shape, lambda b, t: (b,) + zeros)


def _full_spec(shape):
    zeros = (0,) * len(shape)
    return pl.BlockSpec(shape, lambda *_: zeros)


def _lru_specs(w, j):
    return [_whole(w["norm_mix"]), _layer_of(w["lru_w_in"], j), _layer_of(w["lru_conv_w"], j),
            _whole(w["lru_conv_b"]), _layer_of(w["lru_w_a"], j), _whole(w["lru_b_a"]),
            _layer_of(w["lru_w_x"], j), _whole(w["lru_b_x"]), _whole(w["lru_lambda"]),
            _layer_of(w["lru_w_out"], j)]


def _lru_args(w):
    return [w[k] for k in ("norm_mix", "lru_w_in", "lru_conv_w", "lru_conv_b", "lru_w_a", "lru_b_a",
                           "lru_w_x", "lru_b_x", "lru_lambda", "lru_w_out")]


def _ffn_specs(w, i):
    return [_whole(w["norm_ffn"]), _layer_of(w["ffn_w_up"], i), _layer_of(w["ffn_conv_w"], i),
            _whole(w["ffn_conv_b"]), _layer_of(w["ffn_w_down"], i), _whole(w["norm_ple"]),
            _layer_of(w["ple_w_gate"], i), _layer_of(w["ple_w_proj"], i), _whole(w["norm_final"])]


def _ffn_args(w):
    return [w[k] for k in ("norm_ffn", "ffn_w_up", "ffn_conv_w", "ffn_conv_b", "ffn_w_down", "norm_ple",
                           "ple_w_gate", "ple_w_proj", "norm_final")]


def _time_major_perm(bsz, tt):
    r = jnp.arange(bsz * tt)
    src = (r % bsz) * tt + r // bsz
    return (src[:, None] == r[None, :]).astype(_BF16)


def _lru(x, h0, past, w, layer, j):
    width = w["lru_w_out"].shape[1]
    perm = x.ndim == 3
    if perm:
        bsz, t, d = x.shape
        tt = min(_SEQ_TILE // bsz, t)
        assert tt % _SUBLANES == 0 and t % tt == 0 and bsz % _SUBLANES == 0
        rows, grid = bsz * tt, (t // tt,)
        x_spec = pl.BlockSpec((bsz, tt, d), lambda t: (0, t, 0))
        p = _time_major_perm(bsz, tt)
        extra, extra_specs = [p, p.T], [_whole(p), _whole(p)]
    else:
        rows, grid = x.shape[0], (1,)
        x_spec = _full_spec(x.shape)
        extra, extra_specs = [], []
    out_shape = [jax.ShapeDtypeStruct(x.shape, _F32), jax.ShapeDtypeStruct(h0.shape, _F32),
                 jax.ShapeDtypeStruct(past.shape, _F32)]
    return pl.pallas_call(
        functools.partial(_lru_kernel, layer, j, perm),
        grid=grid,
        in_specs=[x_spec, _whole(h0), _whole(past)] + extra_specs + _lru_specs(w, j),
        out_specs=[x_spec, _full_spec(h0.shape), _full_spec(past.shape)],
        out_shape=out_shape,
        scratch_shapes=[pltpu.VMEM((rows, width), _F32)] * 2,
        compiler_params=_params(1),
        name="lru_seq" if perm else "lru_tm",
    )(x, h0, past, *extra, *_lru_args(w))


def _gmlp_seq(x, w, layer, j):
    bsz, t, d = x.shape
    width = w["gm_w_out"].shape[1]
    tm = min(_SEQ_TILE, t)
    return pl.pallas_call(
        functools.partial(_gmlp_seq_kernel, layer, j),
        grid=(bsz, t // tm),
        in_specs=[_row_spec(tm, d), _whole(w["norm_mix"]), _layer_of(w["gm_w_in"], j), _whole(w["gm_ln_g"]),
                  _whole(w["gm_ln_b"]), _layer_of(w["gm_w_s"], j), _layer_of(w["gm_b_s_rows"], j),
                  _layer_of(w["gm_w_out"], j)],
        out_specs=_row_spec(tm, d),
        out_shape=jax.ShapeDtypeStruct((bsz, t, d), _F32),
        scratch_shapes=[pltpu.VMEM((tm, width), _F32)],
        compiler_params=_params(2),
        name="gmlp_seq",
    )(x, w["norm_mix"], w["gm_w_in"], w["gm_ln_g"], w["gm_ln_b"], w["gm_w_s"], w["gm_b_s_rows"],
      w["gm_w_out"])


def _ffn_seq(x, p_all, w, layer, final):
    bsz, t, d = x.shape
    pdim = p_all.shape[-1]
    d_ff = w["ffn_w_down"].shape[1]
    tm = min(_FFN_TILE, t)
    p_spec = pl.BlockSpec((None, None, tm, pdim), lambda b, t: (layer, b, t, 0))
    return pl.pallas_call(
        functools.partial(_ffn_seq_kernel, layer, final),
        grid=(bsz, t // tm),
        in_specs=[_row_spec(tm, d), p_spec] + _ffn_specs(w, layer),
        out_specs=[_row_spec(tm, d), _state_spec((_SUBLANES, d_ff))],
        out_shape=[jax.ShapeDtypeStruct((bsz, t, d), _F32),
                   jax.ShapeDtypeStruct((bsz, _SUBLANES, d_ff), _F32)],
        scratch_shapes=[pltpu.VMEM((tm, d), _F32)],
        compiler_params=_params(2),
        name="ffn_seq",
    )(x, p_all, *_ffn_args(w))


def _gmlp_tm(x2d, n_t, w, layer, j):
    rows, d = x2d.shape
    width = w["gm_w_out"].shape[1]
    out_shape = [jax.ShapeDtypeStruct((rows, d), _F32), jax.ShapeDtypeStruct((rows, width), _F32)]
    return pl.pallas_call(
        functools.partial(_gmlp_tm_kernel, layer, j, n_t),
        grid=(1,),
        in_specs=[_full_spec(x2d.shape), _whole(w["norm_mix"]), _layer_of(w["gm_w_in"], j), _whole(w["gm_ln_g"]),
                  _whole(w["gm_ln_b"]), _layer_of(w["gm_w_s_head"], j), _layer_of(w["gm_b_s_head"], j),
                  _layer_of(w["gm_w_out"], j)],
        out_specs=[_full_spec(s.shape) for s in out_shape],
        out_shape=out_shape,
        scratch_shapes=[pltpu.VMEM((rows, width), _F32)],
        compiler_params=_params(1),
        name="gmlp_tm",
    )(x2d, w["norm_mix"], w["gm_w_in"], w["gm_ln_g"], w["gm_ln_b"], w["gm_w_s_head"], w["gm_b_s_head"],
      w["gm_w_out"])


def _ffn_tm(x2d, p2d, past2d, w, layer, final):
    out_shape = [jax.ShapeDtypeStruct(x2d.shape, _F32), jax.ShapeDtypeStruct(past2d.shape, _F32)]
    return pl.pallas_call(
        functools.partial(_ffn_tm_kernel, layer, final),
        grid=(1,),
        in_specs=[_full_spec(x2d.shape), _full_spec(p2d.shape), _full_spec(past2d.shape)] + _ffn_specs(w, layer),
        out_specs=[_full_spec(s.shape) for s in out_shape],
        out_shape=out_shape,
        scratch_shapes=[pltpu.VMEM(x2d.shape, _F32)],
        compiler_params=_params(1),
        name="ffn_tm",
    )(x2d, p2d, past2d, *_ffn_args(w))


def _to_tm(a):
    b, t, c = a.shape
    return jnp.transpose(a, (1, 0, 2)).reshape(t * b, c)


def _from_tm(a2d, n_t):
    rows, c = a2d.shape
    return jnp.transpose(a2d.reshape(n_t, rows // n_t, c), (1, 0, 2))


def kernel(x_prompt, x_sample, p_prompt, p_sample, state_lru_h, state_lru_conv, state_ffn_conv, norm_mix, norm_ffn, norm_ple, norm_final, lru_w_in, lru_conv_w, lru_conv_b, lru_w_a, lru_b_a, lru_w_x, lru_b_x, lru_lambda, lru_w_out, gm_w_in, gm_ln_g, gm_ln_b, gm_w_s, gm_b_s, gm_w_out, ffn_w_up, ffn_conv_w, ffn_conv_b, ffn_w_down, ple_w_gate, ple_w_proj):
    depth = norm_mix.shape[0]
    bsz, seq, d_model = x_prompt.shape
    dec_b, dec_t, _ = x_sample.shape
    d_ff = ffn_w_down.shape[1]
    groups, chunk = gm_w_s.shape[1], gm_w_s.shape[2]
    gd = gm_w_out.shape[1] // groups
    tm = min(_SEQ_TILE, seq)
    assert seq % tm == 0 and tm % chunk == 0 and tm % _SUBLANES == 0 and d_ff % min(_FFN_CHUNK, d_ff) == 0
    assert _PAST_LEN % chunk == 0 and dec_t <= chunk and dec_b % _SUBLANES == 0
    assert seq % min(_FFN_TILE, seq) == 0

    w = dict(
        norm_mix=norm_mix, norm_ffn=norm_ffn, norm_ple=norm_ple, norm_final=norm_final.reshape(1, -1),
        lru_w_in=lru_w_in.astype(_BF16), lru_conv_w=lru_conv_w, lru_conv_b=lru_conv_b,
        lru_w_a=lru_w_a.astype(_BF16), lru_b_a=lru_b_a, lru_w_x=lru_w_x.astype(_BF16), lru_b_x=lru_b_x,
        lru_lambda=lru_lambda, lru_w_out=lru_w_out.astype(_BF16),
        gm_w_in=gm_w_in.astype(_BF16), gm_ln_g=gm_ln_g, gm_ln_b=gm_ln_b, gm_w_s=gm_w_s,
        gm_w_out=gm_w_out.astype(_BF16),
        gm_b_s_rows=jnp.repeat(jnp.transpose(gm_b_s, (0, 2, 1)), gd, axis=2),
        gm_w_s_head=jnp.repeat(jnp.transpose(gm_w_s[:, :, :dec_t, :dec_t], (0, 2, 3, 1)), gd, axis=3),
        gm_b_s_head=jnp.repeat(jnp.transpose(gm_b_s[:, :, :dec_t], (0, 2, 1)), gd, axis=2),
        ffn_w_up=ffn_w_up.astype(_BF16), ffn_conv_w=ffn_conv_w, ffn_conv_b=ffn_conv_b,
        ffn_w_down=ffn_w_down.astype(_BF16), ple_w_gate=ple_w_gate.astype(_BF16),
        ple_w_proj=ple_w_proj.astype(_BF16),
    )

    hp = x_prompt
    hs = _to_tm(x_sample)
    lru_h_p, lru_conv_p, ffn_conv_p = [], [], []
    lru_h_s, lru_conv_s, ffn_conv_s, gm_v_s = [], [], [], []
    for i in range(depth):
        j = i // 2
        if i % 2 == 0:
            kw = lru_conv_w.shape[1]
            width = lru_w_out.shape[1]
            hp, hlast, tail = _lru(hp, jnp.zeros((bsz, width), _F32), jnp.zeros(((kw - 1) * bsz, width), _F32),
                                   w, i, j)
            lru_h_p.append(hlast)
            lru_conv_p.append(_from_tm(tail, kw - 1))
            hs, hlast, tail = _lru(hs, state_lru_h[j], _to_tm(state_lru_conv[j]), w, i, j)
            lru_h_s.append(hlast)
            lru_conv_s.append(_from_tm(tail, kw - 1))
        else:
            hp = _gmlp_seq(hp, w, i, j)
            hs, v = _gmlp_tm(hs, dec_t, w, i, j)
            gm_v_s.append(_from_tm(v, dec_t))
        kw = ffn_conv_w.shape[1]
        final = i == depth - 1
        hp, tail = _ffn_seq(hp, p_prompt, w, i, final)
        ffn_conv_p.append(tail[:, _SUBLANES - (kw - 1):, :])
        hs, tail = _ffn_tm(hs, _to_tm(p_sample[i]), _to_tm(state_ffn_conv[i]), w, i, final)
        ffn_conv_s.append(_from_tm(tail, kw - 1))

    return (hp, _from_tm(hs, dec_t), jnp.stack(lru_h_p), jnp.stack(lru_conv_p), jnp.stack(ffn_conv_p),
            jnp.stack(lru_h_s), jnp.stack(lru_conv_s), jnp.stack(ffn_conv_s), jnp.stack(gm_v_s))
```

```python
import functools
import math

import jax
import jax.numpy as jnp
from jax import lax
from jax.experimental import pallas as pl
from jax.experimental.pallas import tpu as pltpu

_EPS = 1e-6
_LRU_C = 8.0
_PAST_LEN = 16384
_SUBLANES = 8
_SEQ_TILE = 512
_FFN_TILE = 1024
_FFN_CHUNK = 1024
_PLE_ROWS = 256
_VMEM_LIMIT = 56 * 1024 * 1024

_F32 = jnp.float32
_BF16 = jnp.bfloat16


def _mm(a, b):
    return jnp.dot(a, b, preferred_element_type=_F32)


def _vec(ref, i):
    return ref[i:i + 1, :]


def _to_tm(a):
    b, t, c = a.shape
    return jnp.swapaxes(a, 0, 1).reshape(t * b, c)


def _from_tm(a2d, n_t):
    rows, c = a2d.shape
    return jnp.swapaxes(a2d.reshape(n_t, rows // n_t, c), 0, 1)


def _rmsnorm(x, g):
    ms = jnp.mean(x * x, axis=-1, keepdims=True)
    return (x * lax.rsqrt(ms + _EPS)) * g


def _layernorm(x, g, b):
    mu = jnp.mean(x, axis=-1, keepdims=True)
    xc = x - mu
    y = xc * lax.rsqrt(jnp.mean(xc * xc, axis=-1, keepdims=True) + _EPS)
    return y * g + b


def _gelu(x):
    c = math.sqrt(2.0 / math.pi)
    return x * (0.5 * (1.0 + jnp.tanh(c * (x + 0.044715 * (x * x * x)))))


def _softplus(x):
    return jnp.maximum(x, 0.0) + jnp.log1p(jnp.exp(-jnp.abs(x)))


def _conv_rows(x, prev8, w, b):
    kw = w.shape[0]
    x0 = x[0:_SUBLANES]
    row = lax.broadcasted_iota(jnp.int32, x0.shape, 0)
    y = None
    y0 = None
    for k in range(kw):
        d = kw - 1 - k
        wk = w[k:k + 1]
        if d == 0:
            t, t0 = x * wk, x0 * wk
        else:
            t = pltpu.roll(x, d, 0) * wk
            t0 = jnp.where(row < d, pltpu.roll(prev8, d, 0), pltpu.roll(x0, d, 0)) * wk
        y = t if y is None else y + t
        y0 = t0 if y0 is None else y0 + t0
    return jnp.concatenate([y0, y[_SUBLANES:]], axis=0) + b


def _conv_tm(x, past, w, b):
    kw = w.shape[0]
    rows = x.shape[0]
    rb = past.shape[0] // (kw - 1)
    ext = jnp.concatenate([past, x], axis=0)
    y = ext[0:rows] * w[0:1]
    for k in range(1, kw):
        y = y + ext[k * rb:k * rb + rows] * w[k:k + 1]
    return y + b, ext[rows:]


def _sigmoid(z):
    return 0.5 * jnp.tanh(0.5 * z) + 0.5


def _lru_coeffs(xh, wa, ba, wx, bx, decay):
    xhb = xh.astype(_BF16)
    r = _sigmoid(_mm(xhb, wa) + ba)
    i = _sigmoid(_mm(xhb, wx) + bx)
    m = r * decay
    a = jnp.exp(-m)
    mult = jnp.sqrt(jnp.tanh(m) * (1.0 + a * a))
    return a, mult * (i * xh)


def _scan_slabs(a_s, b_s, carry):
    rb = carry.shape[0]
    n_t = a_s.shape[0] // rb

    def body(t, h):
        rs = pl.ds(pl.multiple_of(t * rb, rb), rb)
        h = a_s[rs, :] * h + b_s[rs, :]
        b_s[rs, :] = h
        return h

    return lax.fori_loop(0, n_t, body, carry, unroll=min(n_t, _SUBLANES))


def _ffn_ple(layer, final, x_ref, p_ref, conv_fn, nf_ref, w_up_ref, cw_ref, cb_ref, w_down_ref,
             np_ref, w_gate_ref, w_proj_ref, nfin_ref, o_ref, acc_s):
    rows = x_ref.shape[0]
    d_ff = w_down_ref.shape[0]
    cn = min(_FFN_CHUNK, d_ff)
    rbk = min(_PLE_ROWS, rows)
    blocks = [slice(r * rbk, (r + 1) * rbk) for r in range(rows // rbk)]
    xn_blocks = [_rmsnorm(x_ref[rs, :], _vec(nf_ref, layer)).astype(_BF16) for rs in blocks]
    xn = jnp.concatenate(xn_blocks, axis=0)
    for j in range(d_ff // cn):
        cs = slice(j * cn, (j + 1) * cn)
        us = slice(d_ff + j * cn, d_ff + (j + 1) * cn)
        if j == 0:
            g = jnp.concatenate([_mm(xb, w_up_ref[:, cs]) for xb in xn_blocks], axis=0)
            u = jnp.concatenate([_mm(xb, w_up_ref[:, us]) for xb in xn_blocks], axis=0)
        else:
            g = _mm(xn, w_up_ref[:, cs])
            u = _mm(xn, w_up_ref[:, us])
        gc = conv_fn(cs, g, cw_ref[:, cs], cb_ref[layer:layer + 1, cs])
        d = _mm((_gelu(gc) * u).astype(_BF16), w_down_ref[cs, :])
        if j == 0:
            acc_s[...] = d
        else:
            acc_s[...] += d
    for rs in blocks:
        h = x_ref[rs, :] + acc_s[rs, :]
        gate = _sigmoid(_mm(_rmsnorm(h, _vec(np_ref, layer)).astype(_BF16), w_gate_ref[...]))
        h = h + gate * _mm(p_ref[rs, :].astype(_BF16), w_proj_ref[...])
        o_ref[rs, :] = _rmsnorm(h, nfin_ref[...]) if final else h


def _lru_kernel(layer, j, perm, x_ref, h0_ref, past_ref, *refs):
    if perm:
        p_ref, pt_ref, *refs = refs
    (g_ref, w_in_ref, cw_ref, cb_ref, wa_ref, ba_ref, wx_ref, bx_ref, lam_ref, w_out_ref,
     o_ref, hlast_ref, tail_ref, tail_s, a_s, b_s) = refs

    @pl.when(pl.program_id(0) == 0)
    def _():
        hlast_ref[...] = h0_ref[...]
        tail_s[...] = _to_tm(past_ref[...])

    heads, dh, _ = wa_ref.shape
    width = heads * dh
    if perm:
        x = x_ref[...].reshape(-1, x_ref.shape[-1])
        xn = _rmsnorm(x, _vec(g_ref, layer)).astype(_BF16)
        xn = _mm(p_ref[...], xn).astype(_BF16)
    else:
        x = _to_tm(x_ref[...])
        xn = _rmsnorm(x, _vec(g_ref, layer)).astype(_BF16)
    decay = _LRU_C * _softplus(-_vec(lam_ref, j))
    xb = _mm(xn, w_in_ref[:, width:])
    gate = _mm(xn, w_in_ref[:, :width])
    xc, tail = _conv_tm(xb, tail_s[...], cw_ref[...], _vec(cb_ref, j))
    tail_s[...] = tail
    tail_ref[...] = _from_tm(tail, tail_ref.shape[1])
    for hh in range(heads):
        sl = slice(hh * dh, (hh + 1) * dh)
        a, b = _lru_coeffs(xc[:, sl], wa_ref[hh], ba_ref[j:j + 1, sl], wx_ref[hh], bx_ref[j:j + 1, sl],
                           decay[:, sl])
        a_s[:, sl] = a
        b_s[:, sl] = b
    hlast_ref[...] = _scan_slabs(a_s, b_s, hlast_ref[...])
    y = (_gelu(gate) * b_s[...]).astype(_BF16)
    if perm:
        y = _mm(pt_ref[...], y).astype(_BF16)
    o_ref[...] = (x + _mm(y, w_out_ref[...])).reshape(o_ref.shape)


def _gmlp_seq_kernel(layer, j, x_ref, g_ref, w_in_ref, lng_ref, lnb_ref, ws_ref, bs_ref, w_out_ref,
                     o_ref, s_s):
    x = x_ref[...]
    rows = x.shape[0]
    groups, chunk, _ = ws_ref.shape
    width = s_s.shape[1]
    gd = width // groups
    xn = _rmsnorm(x, _vec(g_ref, layer)).astype(_BF16)
    n_piece = 2
    pw = width // n_piece
    zv = [_mm(xn, w_in_ref[:, width + k * pw:width + (k + 1) * pw]) for k in range(n_piece)]
    zu = [_mm(xn, w_in_ref[:, k * pw:(k + 1) * pw]) for k in range(n_piece)]
    v = jnp.concatenate([_gelu(z) for z in zv], axis=1)
    v = _layernorm(v, _vec(lng_ref, j), _vec(lnb_ref, j)).astype(_BF16)
    tri = (lax.broadcasted_iota(jnp.int32, (chunk, chunk), 0)
           >= lax.broadcasted_iota(jnp.int32, (chunk, chunk), 1))
    n_c = rows // chunk
    for g in range(groups):
        wg = jnp.where(tri, ws_ref[g], 0.0).astype(_BF16)
        cs = slice(g * gd, (g + 1) * gd)
        vg = jnp.concatenate([v[c * chunk:(c + 1) * chunk, cs] for c in range(n_c)], axis=1)
        sg = _mm(wg, vg)
        for c in range(n_c):
            s_s[c * chunk:(c + 1) * chunk, cs] = sg[:, c * gd:(c + 1) * gd] + bs_ref[:, cs]
    u = jnp.concatenate([_gelu(z) for z in zu], axis=1)
    o_ref[...] = x + _mm((u * s_s[...]).astype(_BF16), w_out_ref[...])


def _ffn_seq_kernel(layer, final, x_ref, p_ref, nf_ref, w_up_ref, cw_ref, cb_ref, w_down_ref,
                    np_ref, w_gate_ref, w_proj_ref, nfin_ref, o_ref, tail_ref, prev_s, acc_s):
    @pl.when(pl.program_id(1) == 0)
    def _():
        prev_s[...] = jnp.zeros_like(prev_s)

    rows = x_ref.shape[0]
    keep = tail_ref.shape[0]

    def conv(cs, g, w, b):
        gc = _conv_rows(g, prev_s[:, cs], w, b)
        prev_s[:, cs] = g[rows - _SUBLANES:rows, :]
        tail_ref[:, cs] = g[rows - keep:rows, :]
        return gc

    _ffn_ple(layer, final, x_ref, p_ref, conv, nf_ref, w_up_ref, cw_ref, cb_ref, w_down_ref,
             np_ref, w_gate_ref, w_proj_ref, nfin_ref, o_ref, acc_s)


def _gmlp_tm_kernel(layer, j, n_t, x_ref, g_ref, w_in_ref, lng_ref, lnb_ref, wsc_ref, bsc_ref, w_out_ref,
                    o_ref, v_ref, s_s):
    x = x_ref[...]
    width = s_s.shape[1]
    bs = x.shape[0] // n_t
    z = _gelu(_mm(_rmsnorm(x, _vec(g_ref, layer)).astype(_BF16), w_in_ref[...]))
    u = z[:, :width]
    v = _layernorm(z[:, width:], _vec(lng_ref, j), _vec(lnb_ref, j))
    v_ref[...] = _from_tm(v, n_t)
    for t in range(n_t):
        s = bsc_ref[t:t + 1, :]
        for q in range(t + 1):
            s = s + wsc_ref[t, q:q + 1, :] * v[q * bs:(q + 1) * bs, :]
        s_s[t * bs:(t + 1) * bs, :] = s
    o_ref[...] = x + _mm((u * s_s[...]).astype(_BF16), w_out_ref[...])


def _ffn_tm_kernel(layer, final, x_ref, p_ref, past_ref, nf_ref, w_up_ref, cw_ref, cb_ref, w_down_ref,
                   np_ref, w_gate_ref, w_proj_ref, nfin_ref, o_ref, tail_ref, p_s, past_s, tail_s, o_s, acc_s):
    n_keep = past_ref.shape[1]
    p_s[...] = _to_tm(p_ref[...])
    past_s[...] = _to_tm(past_ref[...])

    def conv(cs, g, w, b):
        gc, tail = _conv_tm(g, past_s[:, cs], w, b)
        tail_s[:, cs] = tail
        return gc

    _ffn_ple(layer, final, x_ref, p_s, conv, nf_ref, w_up_ref, cw_ref, cb_ref, w_down_ref,
             np_ref, w_gate_ref, w_proj_ref, nfin_ref, o_s if final else o_ref, acc_s)
    tail_ref[...] = _from_tm(tail_s[...], n_keep)
    if final:
        o_ref[...] = _from_tm(o_s[...], o_ref.shape[1])


def _whole(a):
    zeros = (0,) * a.ndim
    return pl.BlockSpec(a.shape, lambda *_: zeros, pipeline_mode=pl.Buffered(1))


def _layer_of(a, i):
    zeros = (0,) * (a.ndim - 1)
    return pl.BlockSpec((None,) + a.shape[1:], lambda *_: (i,) + zeros, pipeline_mode=pl.Buffered(1))


def _params(n_axes):
    return pltpu.CompilerParams(dimension_semantics=("arbitrary",) * n_axes, vmem_limit_bytes=_VMEM_LIMIT)


def _row_spec(tm, d):
    return pl.BlockSpec((None, tm, d), lambda b, t: (b, t, 0))


def _state_spec(shape):
    zeros = (0,) * len(shape)
    return pl.BlockSpec((None,) + shape, lambda b, t: (b,) + zeros)


def _full_spec(shape):
    zeros = (0,) * len(shape)
    return pl.BlockSpec(shape, lambda *_: zeros)


def _lru_specs(w, j):
    return [_whole(w["norm_mix"]), _layer_of(w["lru_w_in"], j), _layer_of(w["lru_conv_w"], j),
            _whole(w["lru_conv_b"]), _layer_of(w["lru_w_a"], j), _whole(w["lru_b_a"]),
            _layer_of(w["lru_w_x"], j), _whole(w["lru_b_x"]), _whole(w["lru_lambda"]),
            _layer_of(w["lru_w_out"], j)]


def _lru_args(w):
    return [w[k] for k in ("norm_mix", "lru_w_in", "lru_conv_w", "lru_conv_b", "lru_w_a", "lru_b_a",
                           "lru_w_x", "lru_b_x", "lru_lambda", "lru_w_out")]


def _ffn_specs(w, i):
    return [_whole(w["norm_ffn"]), _layer_of(w["ffn_w_up"], i), _layer_of(w["ffn_conv_w"], i),
            _whole(w["ffn_conv_b"]), _layer_of(w["ffn_w_down"], i), _whole(w["norm_ple"]),
            _layer_of(w["ple_w_gate"], i), _layer_of(w["ple_w_proj"], i), _whole(w["norm_final"])]


def _ffn_args(w):
    return [w[k] for k in ("norm_ffn", "ffn_w_up", "ffn_conv_w", "ffn_conv_b", "ffn_w_down", "norm_ple",
                           "ple_w_gate", "ple_w_proj", "norm_final")]


def _time_major_perm(bsz, tt):
    r = jnp.arange(bsz * tt)
    src = (r % bsz) * tt + r // bsz
    return (src[:, None] == r[None, :]).astype(_BF16)


def _lru(x, h0, past, w, layer, j, prompt):
    bsz, t, d = x.shape
    width = w["lru_w_out"].shape[1]
    if prompt:
        tt = min(_SEQ_TILE // bsz, t)
        assert tt % _SUBLANES == 0 and t % tt == 0 and bsz % _SUBLANES == 0
        p = _time_major_perm(bsz, tt)
        extra, extra_specs = [p, p.T], [_whole(p), _whole(p)]
        o_shape, o_spec = x.shape, pl.BlockSpec((bsz, tt, d), lambda t: (0, t, 0))
    else:
        tt = t
        extra, extra_specs = [], []
        o_shape, o_spec = (t * bsz, d), _full_spec((t * bsz, d))
    out_shape = [jax.ShapeDtypeStruct(o_shape, _F32), jax.ShapeDtypeStruct(h0.shape, _F32),
                 jax.ShapeDtypeStruct(past.shape, _F32)]
    return pl.pallas_call(
        functools.partial(_lru_kernel, layer, j, prompt),
        grid=(t // tt,),
        in_specs=[pl.BlockSpec((bsz, tt, d), lambda t: (0, t, 0)), _whole(h0), _whole(past)]
        + extra_specs + _lru_specs(w, j),
        out_specs=[o_spec, _full_spec(h0.shape), _full_spec(past.shape)],
        out_shape=out_shape,
        scratch_shapes=[pltpu.VMEM((past.shape[1] * bsz, width), _F32),
                        pltpu.VMEM((bsz * tt, width), _F32), pltpu.VMEM((bsz * tt, width), _F32)],
        compiler_params=_params(1),
        name="lru_seq" if prompt else "lru_tm",
    )(x, h0, past, *extra, *_lru_args(w))


def _gmlp_seq(x, w, layer, j):
    bsz, t, d = x.shape
    width = w["gm_w_out"].shape[1]
    tm = min(_SEQ_TILE, t)
    return pl.pallas_call(
        functools.partial(_gmlp_seq_kernel, layer, j),
        grid=(bsz, t // tm),
        in_specs=[_row_spec(tm, d), _whole(w["norm_mix"]), _layer_of(w["gm_w_in"], j), _whole(w["gm_ln_g"]),
                  _whole(w["gm_ln_b"]), _layer_of(w["gm_w_s"], j), _layer_of(w["gm_b_s_rows"], j),
                  _layer_of(w["gm_w_out"], j)],
        out_specs=_row_spec(tm, d),
        out_shape=jax.ShapeDtypeStruct((bsz, t, d), _F32),
        scratch_shapes=[pltpu.VMEM((tm, width), _F32)],
        compiler_params=_params(2),
        name="gmlp_seq",
    )(x, w["norm_mix"], w["gm_w_in"], w["gm_ln_g"], w["gm_ln_b"], w["gm_w_s"], w["gm_b_s_rows"],
      w["gm_w_out"])


def _ffn_seq(x, p_all, w, layer, final, n_keep):
    bsz, t, d = x.shape
    pdim = p_all.shape[-1]
    d_ff = w["ffn_w_down"].shape[1]
    tm = min(_FFN_TILE, t)
    p_spec = pl.BlockSpec((None, None, tm, pdim), lambda b, t: (layer, b, t, 0))
    return pl.pallas_call(
        functools.partial(_ffn_seq_kernel, layer, final),
        grid=(bsz, t // tm),
        in_specs=[_row_spec(tm, d), p_spec] + _ffn_specs(w, layer),
        out_specs=[_row_spec(tm, d), _state_spec((n_keep, d_ff))],
        out_shape=[jax.ShapeDtypeStruct((bsz, t, d), _F32),
                   jax.ShapeDtypeStruct((bsz, n_keep, d_ff), _F32)],
        scratch_shapes=[pltpu.VMEM((_SUBLANES, d_ff), _F32), pltpu.VMEM((tm, d), _F32)],
        compiler_params=_params(2),
        name="ffn_seq",
    )(x, p_all, *_ffn_args(w))


def _gmlp_tm(x2d, n_t, w, layer, j):
    rows, d = x2d.shape
    width = w["gm_w_out"].shape[1]
    out_shape = [jax.ShapeDtypeStruct((rows, d), _F32), jax.ShapeDtypeStruct((rows // n_t, n_t, width), _F32)]
    return pl.pallas_call(
        functools.partial(_gmlp_tm_kernel, layer, j, n_t),
        grid=(1,),
        in_specs=[_full_spec(x2d.shape), _whole(w["norm_mix"]), _layer_of(w["gm_w_in"], j), _whole(w["gm_ln_g"]),
                  _whole(w["gm_ln_b"]), _layer_of(w["gm_w_s_head"], j), _layer_of(w["gm_b_s_head"], j),
                  _layer_of(w["gm_w_out"], j)],
        out_specs=[_full_spec(s.shape) for s in out_shape],
        out_shape=out_shape,
        scratch_shapes=[pltpu.VMEM((rows, width), _F32)],
        compiler_params=_params(1),
        name="gmlp_tm",
    )(x2d, w["norm_mix"], w["gm_w_in"], w["gm_ln_g"], w["gm_ln_b"], w["gm_w_s_head"], w["gm_b_s_head"],
      w["gm_w_out"])


def _ffn_tm(x2d, p_all, past_all, w, layer, final):
    rows, d = x2d.shape
    _, bsz, n_t, pdim = p_all.shape
    n_keep, d_ff = past_all.shape[2], past_all.shape[3]
    o_shape = (bsz, n_t, d) if final else (rows, d)
    out_shape = [jax.ShapeDtypeStruct(o_shape, _F32), jax.ShapeDtypeStruct(past_all.shape[1:], _F32)]
    return pl.pallas_call(
        functools.partial(_ffn_tm_kernel, layer, final),
        grid=(1,),
        in_specs=[_full_spec(x2d.shape), _layer_of(p_all, layer), _layer_of(past_all, layer)]
        + _ffn_specs(w, layer),
        out_specs=[_full_spec(s.shape) for s in out_shape],
        out_shape=out_shape,
        scratch_shapes=[pltpu.VMEM((rows, pdim), _F32), pltpu.VMEM((n_keep * bsz, d_ff), _F32),
                        pltpu.VMEM((n_keep * bsz, d_ff), _F32), pltpu.VMEM((rows, d), _F32),
                        pltpu.VMEM((rows, d), _F32)],
        compiler_params=_params(1),
        name="ffn_tm",
    )(x2d, p_all, past_all, *_ffn_args(w))


def kernel(x_prompt, x_sample, p_prompt, p_sample, state_lru_h, state_lru_conv, state_ffn_conv, norm_mix, norm_ffn, norm_ple, norm_final, lru_w_in, lru_conv_w, lru_conv_b, lru_w_a, lru_b_a, lru_w_x, lru_b_x, lru_lambda, lru_w_out, gm_w_in, gm_ln_g, gm_ln_b, gm_w_s, gm_b_s, gm_w_out, ffn_w_up, ffn_conv_w, ffn_conv_b, ffn_w_down, ple_w_gate, ple_w_proj):
    depth = norm_mix.shape[0]
    bsz, seq, d_model = x_prompt.shape
    dec_b, dec_t, _ = x_sample.shape
    d_ff = ffn_w_down.shape[1]
    groups, chunk = gm_w_s.shape[1], gm_w_s.shape[2]
    gd = gm_w_out.shape[1] // groups
    tm = min(_SEQ_TILE, seq)
    assert seq % tm == 0 and tm % chunk == 0 and tm % _SUBLANES == 0 and d_ff % min(_FFN_CHUNK, d_ff) == 0
    assert _PAST_LEN % chunk == 0 and dec_t <= chunk and dec_b % _SUBLANES == 0
    assert seq % min(_FFN_TILE, seq) == 0 and depth % 2 == 0

    w = dict(
        norm_mix=norm_mix, norm_ffn=norm_ffn, norm_ple=norm_ple, norm_final=norm_final.reshape(1, -1),
        lru_w_in=lru_w_in.astype(_BF16), lru_conv_w=lru_conv_w, lru_conv_b=lru_conv_b,
        lru_w_a=lru_w_a.astype(_BF16), lru_b_a=lru_b_a, lru_w_x=lru_w_x.astype(_BF16), lru_b_x=lru_b_x,
        lru_lambda=lru_lambda, lru_w_out=lru_w_out.astype(_BF16),
        gm_w_in=gm_w_in.astype(_BF16), gm_ln_g=gm_ln_g, gm_ln_b=gm_ln_b, gm_w_s=gm_w_s,
        gm_w_out=gm_w_out.astype(_BF16),
        gm_b_s_rows=jnp.repeat(jnp.transpose(gm_b_s, (0, 2, 1)), gd, axis=2),
        gm_w_s_head=jnp.repeat(jnp.transpose(gm_w_s[:, :, :dec_t, :dec_t], (0, 2, 3, 1)), gd, axis=3),
        gm_b_s_head=jnp.repeat(jnp.transpose(gm_b_s[:, :, :dec_t], (0, 2, 1)), gd, axis=2),
        ffn_w_up=ffn_w_up.astype(_BF16), ffn_conv_w=ffn_conv_w, ffn_conv_b=ffn_conv_b,
        ffn_w_down=ffn_w_down.astype(_BF16), ple_w_gate=ple_w_gate.astype(_BF16),
        ple_w_proj=ple_w_proj.astype(_BF16),
    )

    hp = x_prompt
    hs = x_sample
    lru_h_p, lru_conv_p, ffn_conv_p = [], [], []
    lru_h_s, lru_conv_s, ffn_conv_s, gm_v_s = [], [], [], []
    for i in range(depth):
        j = i // 2
        if i % 2 == 0:
            kw = lru_conv_w.shape[1]
            width = lru_w_out.shape[1]
            hp, hlast, tail = _lru(hp, jnp.zeros((bsz, width), _F32), jnp.zeros((bsz, kw - 1, width), _F32),
                                   w, i, j, True)
            lru_h_p.append(hlast)
            lru_conv_p.append(tail)
            if hs.ndim == 2:
                hs = _from_tm(hs, dec_t)
            hs, hlast, tail = _lru(hs, state_lru_h[j], state_lru_conv[j], w, i, j, False)
            lru_h_s.append(hlast)
            lru_conv_s.append(tail)
        else:
            hp = _gmlp_seq(hp, w, i, j)
            hs, v = _gmlp_tm(hs, dec_t, w, i, j)
            gm_v_s.append(v)
        kw = ffn_conv_w.shape[1]
        final = i == depth - 1
        hp, tail = _ffn_seq(hp, p_prompt, w, i, final, kw - 1)
        ffn_conv_p.append(tail)
        hs, tail = _ffn_tm(hs, p_sample, state_ffn_conv, w, i, final)
        ffn_conv_s.append(tail)

    return (hp, hs, jnp.stack(lru_h_p), jnp.stack(lru_conv_p), jnp.stack(ffn_conv_p),
            jnp.stack(lru_h_s), jnp.stack(lru_conv_s), jnp.stack(ffn_conv_s), jnp.stack(gm_v_s))
```

```python
import functools
import math

import jax
import jax.numpy as jnp
from jax import lax
from jax.experimental import pallas as pl
from jax.experimental.pallas import tpu as pltpu

_EPS = 1e-6
_LRU_C = 8.0
_PAST_LEN = 16384
_SUBLANES = 8
_LRU_TILE = 1024
_GMLP_TILE = 1024
_FFN_TILE = 1024
_FFN_CHUNK = 1024
_PLE_ROWS = 256
_VMEM_LIMIT = 56 * 1024 * 1024

_F32 = jnp.float32
_BF16 = jnp.bfloat16


def _mm(a, b):
    return jnp.dot(a, b, preferred_element_type=_F32)


def _vec(ref, i):
    return ref[i:i + 1, :]


def _to_tm(a):
    b, t, c = a.shape
    return jnp.swapaxes(a, 0, 1).reshape(t * b, c)


def _from_tm(a2d, n_t):
    rows, c = a2d.shape
    return jnp.swapaxes(a2d.reshape(n_t, rows // n_t, c), 0, 1)


def _rmsnorm(x, g):
    ms = jnp.mean(x * x, axis=-1, keepdims=True)
    return (x * lax.rsqrt(ms + _EPS)) * g


def _layernorm(x, g, b):
    mu = jnp.mean(x, axis=-1, keepdims=True)
    xc = x - mu
    y = xc * lax.rsqrt(jnp.mean(xc * xc, axis=-1, keepdims=True) + _EPS)
    return y * g + b


def _gelu(x):
    c = math.sqrt(2.0 / math.pi)
    return x * (0.5 * (1.0 + jnp.tanh(c * (x + 0.044715 * (x * x * x)))))


def _softplus(x):
    return jnp.maximum(x, 0.0) + jnp.log1p(jnp.exp(-jnp.abs(x)))


def _conv_rows(x, prev8, w, b):
    kw = w.shape[0]
    x0 = x[0:_SUBLANES]
    row = lax.broadcasted_iota(jnp.int32, x0.shape, 0)
    y = None
    y0 = None
    for k in range(kw):
        d = kw - 1 - k
        wk = w[k:k + 1]
        if d == 0:
            t, t0 = x * wk, x0 * wk
        else:
            t = pltpu.roll(x, d, 0) * wk
            t0 = jnp.where(row < d, pltpu.roll(prev8, d, 0), pltpu.roll(x0, d, 0)) * wk
        y = t if y is None else y + t
        y0 = t0 if y0 is None else y0 + t0
    return jnp.concatenate([y0, y[_SUBLANES:]], axis=0) + b


def _conv_tm(x, past, w, b):
    kw = w.shape[0]
    rows = x.shape[0]
    rb = past.shape[0] // (kw - 1)
    ext = jnp.concatenate([past, x], axis=0)
    y = ext[0:rows] * w[0:1]
    for k in range(1, kw):
        y = y + ext[k * rb:k * rb + rows] * w[k:k + 1]
    return y + b, ext[rows:]


def _sigmoid(z):
    return 0.5 * jnp.tanh(0.5 * z) + 0.5


def _lru_coeffs(xh, wa, ba, wx, bx, decay):
    xhb = xh.astype(_BF16)
    r = _sigmoid(_mm(xhb, wa) + ba)
    i = _sigmoid(_mm(xhb, wx) + bx)
    m = r * decay
    a = jnp.exp(-m)
    mult = jnp.sqrt(jnp.tanh(m) * (1.0 + a * a))
    return a, mult * (i * xh)


def _scan_slabs(a_s, b_s, carry):
    rb = carry.shape[0]
    n_t = a_s.shape[0] // rb

    def body(t, h):
        rs = pl.ds(pl.multiple_of(t * rb, rb), rb)
        h = a_s[rs, :] * h + b_s[rs, :]
        b_s[rs, :] = h
        return h

    return lax.fori_loop(0, n_t, body, carry, unroll=min(n_t, _SUBLANES))


def _ffn_ple(layer, final, x_ref, p_ref, conv_fn, nf_ref, w_up_ref, cw_ref, cb_ref, w_down_ref,
             np_ref, w_gate_ref, w_proj_ref, nfin_ref, o_ref, acc_s):
    rows = x_ref.shape[0]
    d_ff = w_down_ref.shape[0]
    cn = min(_FFN_CHUNK, d_ff)
    rbk = min(_PLE_ROWS, rows)
    blocks = [slice(r * rbk, (r + 1) * rbk) for r in range(rows // rbk)]
    xn_blocks = [_rmsnorm(x_ref[rs, :], _vec(nf_ref, layer)).astype(_BF16) for rs in blocks]
    xn = jnp.concatenate(xn_blocks, axis=0)
    for j in range(d_ff // cn):
        cs = slice(j * cn, (j + 1) * cn)
        us = slice(d_ff + j * cn, d_ff + (j + 1) * cn)
        if j == 0:
            g = jnp.concatenate([_mm(xb, w_up_ref[:, cs]) for xb in xn_blocks], axis=0)
            u = jnp.concatenate([_mm(xb, w_up_ref[:, us]) for xb in xn_blocks], axis=0)
        else:
            g = _mm(xn, w_up_ref[:, cs])
            u = _mm(xn, w_up_ref[:, us])
        gc = conv_fn(cs, g, cw_ref[:, cs], cb_ref[layer:layer + 1, cs])
        d = _mm((_gelu(gc) * u).astype(_BF16), w_down_ref[cs, :])
        if j == 0:
            acc_s[...] = d
        else:
            acc_s[...] += d
    for rs in blocks:
        h = x_ref[rs, :] + acc_s[rs, :]
        gate = _sigmoid(_mm(_rmsnorm(h, _vec(np_ref, layer)).astype(_BF16), w_gate_ref[...]))
        h = h + gate * _mm(p_ref[rs, :].astype(_BF16), w_proj_ref[...])
        o_ref[rs, :] = _rmsnorm(h, nfin_ref[...]) if final else h


def _lru_kernel(layer, j, x_ref, h0_ref, past_ref, g_ref, w_in_ref, cw_ref, cb_ref, wa_ref, ba_ref,
                wx_ref, bx_ref, lam_ref, w_out_ref, o_ref, hlast_ref, tail_ref, tail_s, a_s, b_s):
    @pl.when(pl.program_id(0) == 0)
    def _():
        hlast_ref[...] = h0_ref[...]
        tail_s[...] = _to_tm(past_ref[...])

    heads, dh, _ = wa_ref.shape
    width = heads * dh
    x = _to_tm(x_ref[...])
    xn = _rmsnorm(x, _vec(g_ref, layer)).astype(_BF16)
    decay = _LRU_C * _softplus(-_vec(lam_ref, j))
    xb = _mm(xn, w_in_ref[:, width:])
    gate = _mm(xn, w_in_ref[:, :width])
    xc, tail = _conv_tm(xb, tail_s[...], cw_ref[...], _vec(cb_ref, j))
    tail_s[...] = tail
    tail_ref[...] = _from_tm(tail, tail_ref.shape[1])
    for hh in range(heads):
        sl = slice(hh * dh, (hh + 1) * dh)
        a, b = _lru_coeffs(xc[:, sl], wa_ref[hh], ba_ref[j:j + 1, sl], wx_ref[hh], bx_ref[j:j + 1, sl],
                           decay[:, sl])
        a_s[:, sl] = a
        b_s[:, sl] = b
    hlast_ref[...] = _scan_slabs(a_s, b_s, hlast_ref[...])
    y = _mm((_gelu(gate) * b_s[...]).astype(_BF16), w_out_ref[...])
    if len(o_ref.shape) == 3:
        o_ref[...] = x_ref[...] + _from_tm(y, o_ref.shape[1])
    else:
        o_ref[...] = x + y


def _gmlp_seq_kernel(layer, j, x_ref, g_ref, w_in_ref, lng_ref, lnb_ref, ws_ref, bs_ref, w_out_ref,
                     o_ref, s_s):
    x = x_ref[...]
    rows = x.shape[0]
    groups, chunk, _ = ws_ref.shape
    width = s_s.shape[1]
    gd = width // groups
    xn = _rmsnorm(x, _vec(g_ref, layer)).astype(_BF16)
    n_piece = 2
    pw = width // n_piece
    zv = [_mm(xn, w_in_ref[:, width + k * pw:width + (k + 1) * pw]) for k in range(n_piece)]
    zu = [_mm(xn, w_in_ref[:, k * pw:(k + 1) * pw]) for k in range(n_piece)]
    v = jnp.concatenate([_gelu(z) for z in zv], axis=1)
    v = _layernorm(v, _vec(lng_ref, j), _vec(lnb_ref, j)).astype(_BF16)
    tri = (lax.broadcasted_iota(jnp.int32, (chunk, chunk), 0)
           >= lax.broadcasted_iota(jnp.int32, (chunk, chunk), 1))
    n_c = rows // chunk
    for g in range(groups):
        wg = jnp.where(tri, ws_ref[g], 0.0).astype(_BF16)
        cs = slice(g * gd, (g + 1) * gd)
        vg = jnp.concatenate([v[c * chunk:(c + 1) * chunk, cs] for c in range(n_c)], axis=1)
        sg = _mm(wg, vg)
        for c in range(n_c):
            s_s[c * chunk:(c + 1) * chunk, cs] = sg[:, c * gd:(c + 1) * gd] + bs_ref[:, cs]
    u = jnp.concatenate([_gelu(z) for z in zu], axis=1)
    o_ref[...] = x + _mm((u * s_s[...]).astype(_BF16), w_out_ref[...])


def _ffn_seq_kernel(layer, final, x_ref, p_ref, nf_ref, w_up_ref, cw_ref, cb_ref, w_down_ref,
                    np_ref, w_gate_ref, w_proj_ref, nfin_ref, o_ref, tail_ref, prev_s, acc_s):
    @pl.when(pl.program_id(1) == 0)
    def _():
        prev_s[...] = jnp.zeros_like(prev_s)

    rows = x_ref.shape[0]
    keep = tail_ref.shape[0]

    def conv(cs, g, w, b):
        gc = _conv_rows(g, prev_s[:, cs], w, b)
        prev_s[:, cs] = g[rows - _SUBLANES:rows, :]
        tail_ref[:, cs] = g[rows - keep:rows, :]
        return gc

    _ffn_ple(layer, final, x_ref, p_ref, conv, nf_ref, w_up_ref, cw_ref, cb_ref, w_down_ref,
             np_ref, w_gate_ref, w_proj_ref, nfin_ref, o_ref, acc_s)


def _gmlp_tm_kernel(layer, j, n_t, x_ref, g_ref, w_in_ref, lng_ref, lnb_ref, wsc_ref, bsc_ref, w_out_ref,
                    o_ref, v_ref, s_s):
    x = x_ref[...]
    width = s_s.shape[1]
    bs = x.shape[0] // n_t
    z = _gelu(_mm(_rmsnorm(x, _vec(g_ref, layer)).astype(_BF16), w_in_ref[...]))
    u = z[:, :width]
    v = _layernorm(z[:, width:], _vec(lng_ref, j), _vec(lnb_ref, j))
    v_ref[...] = _from_tm(v, n_t)
    for t in range(n_t):
        s = bsc_ref[t:t + 1, :]
        for q in range(t + 1):
            s = s + wsc_ref[t, q:q + 1, :] * v[q * bs:(q + 1) * bs, :]
        s_s[t * bs:(t + 1) * bs, :] = s
    o_ref[...] = x + _mm((u * s_s[...]).astype(_BF16), w_out_ref[...])


def _ffn_tm_kernel(layer, final, x_ref, p_ref, past_ref, nf_ref, w_up_ref, cw_ref, cb_ref, w_down_ref,
                   np_ref, w_gate_ref, w_proj_ref, nfin_ref, o_ref, tail_ref, p_s, past_s, tail_s, o_s, acc_s):
    n_keep = past_ref.shape[1]
    p_s[...] = _to_tm(p_ref[...])
    past_s[...] = _to_tm(past_ref[...])

    def conv(cs, g, w, b):
        gc, tail = _conv_tm(g, past_s[:, cs], w, b)
        tail_s[:, cs] = tail
        return gc

    _ffn_ple(layer, final, x_ref, p_s, conv, nf_ref, w_up_ref, cw_ref, cb_ref, w_down_ref,
             np_ref, w_gate_ref, w_proj_ref, nfin_ref, o_s if final else o_ref, acc_s)
    tail_ref[...] = _from_tm(tail_s[...], n_keep)
    if final:
        o_ref[...] = _from_tm(o_s[...], o_ref.shape[1])


def _whole(a):
    zeros = (0,) * a.ndim
    return pl.BlockSpec(a.shape, lambda *_: zeros, pipeline_mode=pl.Buffered(1))


def _layer_of(a, i):
    zeros = (0,) * (a.ndim - 1)
    return pl.BlockSpec((None,) + a.shape[1:], lambda *_: (i,) + zeros, pipeline_mode=pl.Buffered(1))


def _params(n_axes):
    return pltpu.CompilerParams(dimension_semantics=("arbitrary",) * n_axes, vmem_limit_bytes=_VMEM_LIMIT)


def _row_spec(tm, d):
    return pl.BlockSpec((None, tm, d), lambda b, t: (b, t, 0))


def _state_spec(shape):
    zeros = (0,) * len(shape)
    return pl.BlockSpec((None,) + shape, lambda b, t: (b,) + zeros)


def _full_spec(shape):
    zeros = (0,) * len(shape)
    return pl.BlockSpec(shape, lambda *_: zeros)


def _lru_specs(w, j):
    return [_whole(w["norm_mix"]), _layer_of(w["lru_w_in"], j), _layer_of(w["lru_conv_w"], j),
            _whole(w["lru_conv_b"]), _layer_of(w["lru_w_a"], j), _whole(w["lru_b_a"]),
            _layer_of(w["lru_w_x"], j), _whole(w["lru_b_x"]), _whole(w["lru_lambda"]),
            _layer_of(w["lru_w_out"], j)]


def _lru_args(w):
    return [w[k] for k in ("norm_mix", "lru_w_in", "lru_conv_w", "lru_conv_b", "lru_w_a", "lru_b_a",
                           "lru_w_x", "lru_b_x", "lru_lambda", "lru_w_out")]


def _ffn_specs(w, i):
    return [_whole(w["norm_ffn"]), _layer_of(w["ffn_w_up"], i), _layer_of(w["ffn_conv_w"], i),
            _whole(w["ffn_conv_b"]), _layer_of(w["ffn_w_down"], i), _whole(w["norm_ple"]),
            _layer_of(w["ple_w_gate"], i), _layer_of(w["ple_w_proj"], i), _whole(w["norm_final"])]


def _ffn_args(w):
    return [w[k] for k in ("norm_ffn", "ffn_w_up", "ffn_conv_w", "ffn_conv_b", "ffn_w_down", "norm_ple",
                           "ple_w_gate", "ple_w_proj", "norm_final")]


def _lru(x, h0, past, w, layer, j, prompt):
    bsz, t, d = x.shape
    width = w["lru_w_out"].shape[1]
    if prompt:
        tt = min(_LRU_TILE // bsz, t)
        assert tt % _SUBLANES == 0 and t % tt == 0 and bsz % _SUBLANES == 0
        o_shape, o_spec = x.shape, pl.BlockSpec((bsz, tt, d), lambda t: (0, t, 0))
    else:
        tt = t
        o_shape, o_spec = (t * bsz, d), _full_spec((t * bsz, d))
    out_shape = [jax.ShapeDtypeStruct(o_shape, _F32), jax.ShapeDtypeStruct(h0.shape, _F32),
                 jax.ShapeDtypeStruct(past.shape, _F32)]
    return pl.pallas_call(
        functools.partial(_lru_kernel, layer, j),
        grid=(t // tt,),
        in_specs=[pl.BlockSpec((bsz, tt, d), lambda t: (0, t, 0)), _whole(h0), _whole(past)]
        + _lru_specs(w, j),
        out_specs=[o_spec, _full_spec(h0.shape), _full_spec(past.shape)],
        out_shape=out_shape,
        scratch_shapes=[pltpu.VMEM((past.shape[1] * bsz, width), _F32),
                        pltpu.VMEM((bsz * tt, width), _F32), pltpu.VMEM((bsz * tt, width), _F32)],
        compiler_params=_params(1),
        name="lru_seq" if prompt else "lru_tm",
    )(x, h0, past, *_lru_args(w))


def _gmlp_seq(x, w, layer, j):
    bsz, t, d = x.shape
    width = w["gm_w_out"].shape[1]
    tm = min(_GMLP_TILE, t)
    return pl.pallas_call(
        functools.partial(_gmlp_seq_kernel, layer, j),
        grid=(bsz, t // tm),
        in_specs=[_row_spec(tm, d), _whole(w["norm_mix"]), _layer_of(w["gm_w_in"], j), _whole(w["gm_ln_g"]),
                  _whole(w["gm_ln_b"]), _layer_of(w["gm_w_s"], j), _layer_of(w["gm_b_s_rows"], j),
                  _layer_of(w["gm_w_out"], j)],
        out_specs=_row_spec(tm, d),
        out_shape=jax.ShapeDtypeStruct((bsz, t, d), _F32),
        scratch_shapes=[pltpu.VMEM((tm, width), _F32)],
        compiler_params=_params(2),
        name="gmlp_seq",
    )(x, w["norm_mix"], w["gm_w_in"], w["gm_ln_g"], w["gm_ln_b"], w["gm_w_s"], w["gm_b_s_rows"],
      w["gm_w_out"])


def _ffn_seq(x, p_all, w, layer, final, n_keep):
    bsz, t, d = x.shape
    pdim = p_all.shape[-1]
    d_ff = w["ffn_w_down"].shape[1]
    tm = min(_FFN_TILE, t)
    p_spec = pl.BlockSpec((None, None, tm, pdim), lambda b, t: (layer, b, t, 0))
    return pl.pallas_call(
        functools.partial(_ffn_seq_kernel, layer, final),
        grid=(bsz, t // tm),
        in_specs=[_row_spec(tm, d), p_spec] + _ffn_specs(w, layer),
        out_specs=[_row_spec(tm, d), _state_spec((n_keep, d_ff))],
        out_shape=[jax.ShapeDtypeStruct((bsz, t, d), _F32),
                   jax.ShapeDtypeStruct((bsz, n_keep, d_ff), _F32)],
        scratch_shapes=[pltpu.VMEM((_SUBLANES, d_ff), _F32), pltpu.VMEM((tm, d), _F32)],
        compiler_params=_params(2),
        name="ffn_seq",
    )(x, p_all, *_ffn_args(w))


def _gmlp_tm(x2d, n_t, w, layer, j):
    rows, d = x2d.shape
    width = w["gm_w_out"].shape[1]
    out_shape = [jax.ShapeDtypeStruct((rows, d), _F32), jax.ShapeDtypeStruct((rows // n_t, n_t, width), _F32)]
    return pl.pallas_call(
        functools.partial(_gmlp_tm_kernel, layer, j, n_t),
        grid=(1,),
        in_specs=[_full_spec(x2d.shape), _whole(w["norm_mix"]), _layer_of(w["gm_w_in"], j), _whole(w["gm_ln_g"]),
                  _whole(w["gm_ln_b"]), _layer_of(w["gm_w_s_head"], j), _layer_of(w["gm_b_s_head"], j),
                  _layer_of(w["gm_w_out"], j)],
        out_specs=[_full_spec(s.shape) for s in out_shape],
        out_shape=out_shape,
        scratch_shapes=[pltpu.VMEM((rows, width), _F32)],
        compiler_params=_params(1),
        name="gmlp_tm",
    )(x2d, w["norm_mix"], w["gm_w_in"], w["gm_ln_g"], w["gm_ln_b"], w["gm_w_s_head"], w["gm_b_s_head"],
      w["gm_w_out"])


def _ffn_tm(x2d, p_all, past_all, w, layer, final):
    rows, d = x2d.shape
    _, bsz, n_t, pdim = p_all.shape
    n_keep, d_ff = past_all.shape[2], past_all.shape[3]
    o_shape = (bsz, n_t, d) if final else (rows, d)
    out_shape = [jax.ShapeDtypeStruct(o_shape, _F32), jax.ShapeDtypeStruct(past_all.shape[1:], _F32)]
    return pl.pallas_call(
        functools.partial(_ffn_tm_kernel, layer, final),
        grid=(1,),
        in_specs=[_full_spec(x2d.shape), _layer_of(p_all, layer), _layer_of(past_all, layer)]
        + _ffn_specs(w, layer),
        out_specs=[_full_spec(s.shape) for s in out_shape],
        out_shape=out_shape,
        scratch_shapes=[pltpu.VMEM((rows, pdim), _F32), pltpu.VMEM((n_keep * bsz, d_ff), _F32),
                        pltpu.VMEM((n_keep * bsz, d_ff), _F32), pltpu.VMEM((rows, d), _F32),
                        pltpu.VMEM((rows, d), _F32)],
        compiler_params=_params(1),
        name="ffn_tm",
    )(x2d, p_all, past_all, *_ffn_args(w))


def kernel(x_prompt, x_sample, p_prompt, p_sample, state_lru_h, state_lru_conv, state_ffn_conv, norm_mix, norm_ffn, norm_ple, norm_final, lru_w_in, lru_conv_w, lru_conv_b, lru_w_a, lru_b_a, lru_w_x, lru_b_x, lru_lambda, lru_w_out, gm_w_in, gm_ln_g, gm_ln_b, gm_w_s, gm_b_s, gm_w_out, ffn_w_up, ffn_conv_w, ffn_conv_b, ffn_w_down, ple_w_gate, ple_w_proj):
    depth = norm_mix.shape[0]
    bsz, seq, d_model = x_prompt.shape
    dec_b, dec_t, _ = x_sample.shape
    d_ff = ffn_w_down.shape[1]
    groups, chunk = gm_w_s.shape[1], gm_w_s.shape[2]
    gd = gm_w_out.shape[1] // groups
    tm = min(_GMLP_TILE, seq)
    assert seq % tm == 0 and tm % chunk == 0 and tm % _SUBLANES == 0 and d_ff % min(_FFN_CHUNK, d_ff) == 0
    assert _PAST_LEN % chunk == 0 and dec_t <= chunk and dec_b % _SUBLANES == 0
    assert seq % min(_FFN_TILE, seq) == 0 and depth % 2 == 0

    w = dict(
        norm_mix=norm_mix, norm_ffn=norm_ffn, norm_ple=norm_ple, norm_final=norm_final.reshape(1, -1),
        lru_w_in=lru_w_in.astype(_BF16), lru_conv_w=lru_conv_w, lru_conv_b=lru_conv_b,
        lru_w_a=lru_w_a.astype(_BF16), lru_b_a=lru_b_a, lru_w_x=lru_w_x.astype(_BF16), lru_b_x=lru_b_x,
        lru_lambda=lru_lambda, lru_w_out=lru_w_out.astype(_BF16),
        gm_w_in=gm_w_in.astype(_BF16), gm_ln_g=gm_ln_g, gm_ln_b=gm_ln_b, gm_w_s=gm_w_s,
        gm_w_out=gm_w_out.astype(_BF16),
        gm_b_s_rows=jnp.repeat(jnp.transpose(gm_b_s, (0, 2, 1)), gd, axis=2),
        gm_w_s_head=jnp.repeat(jnp.transpose(gm_w_s[:, :, :dec_t, :dec_t], (0, 2, 3, 1)), gd, axis=3),
        gm_b_s_head=jnp.repeat(jnp.transpose(gm_b_s[:, :, :dec_t], (0, 2, 1)), gd, axis=2),
        ffn_w_up=ffn_w_up.astype(_BF16), ffn_conv_w=ffn_conv_w, ffn_conv_b=ffn_conv_b,
        ffn_w_down=ffn_w_down.astype(_BF16), ple_w_gate=ple_w_gate.astype(_BF16),
        ple_w_proj=ple_w_proj.astype(_BF16),
    )

    hp = x_prompt
    hs = x_sample
    lru_h_p, lru_conv_p, ffn_conv_p = [], [], []
    lru_h_s, lru_conv_s, ffn_conv_s, gm_v_s = [], [], [], []
    for i in range(depth):
        j = i // 2
        if i % 2 == 0:
            kw = lru_conv_w.shape[1]
            width = lru_w_out.shape[1]
            hp, hlast, tail = _lru(hp, jnp.zeros((bsz, width), _F32), jnp.zeros((bsz, kw - 1, width), _F32),
                                   w, i, j, True)
            lru_h_p.append(hlast)
            lru_conv_p.append(tail)
            if hs.ndim == 2:
                hs = _from_tm(hs, dec_t)
            hs, hlast, tail = _lru(hs, state_lru_h[j], state_lru_conv[j], w, i, j, False)
            lru_h_s.append(hlast)
            lru_conv_s.append(tail)
        else:
            hp = _gmlp_seq(hp, w, i, j)
            hs, v = _gmlp_tm(hs, dec_t, w, i, j)
            gm_v_s.append(v)
        kw = ffn_conv_w.shape[1]
        final = i == depth - 1
        hp, tail = _ffn_seq(hp, p_prompt, w, i, final, kw - 1)
        ffn_conv_p.append(tail)
        hs, tail = _ffn_tm(hs, p_sample, state_ffn_conv, w, i, final)
        ffn_conv_s.append(tail)

    def stack(xs):
        return xs[0][None] if len(xs) == 1 else jnp.stack(xs)

    return (hp, hs, stack(lru_h_p), stack(lru_conv_p), stack(ffn_conv_p),
            stack(lru_h_s), stack(lru_conv_s), stack(ffn_conv_s), stack(gm_v_s))
```

```python
import functools
import math

import jax
import jax.numpy as jnp
from jax import lax
from jax.experimental import pallas as pl
from jax.experimental.pallas import tpu as pltpu

_EPS = 1e-6
_LRU_C = 8.0
_PAST_LEN = 16384
_SUBLANES = 8
_LRU_TILE = 1024
_GMLP_TILE = 1024
_FFN_TILE = 1024
_FFN_CHUNK = 1024
_PLE_ROWS = 256
_LRU_BLOCKS = 4
_VMEM_LIMIT = 56 * 1024 * 1024

_F32 = jnp.float32
_BF16 = jnp.bfloat16


def _mm(a, b):
    return jnp.dot(a, b, preferred_element_type=_F32)


def _vec(ref, i):
    return ref[i:i + 1, :]


def _to_tm(a):
    b, t, c = a.shape
    return jnp.swapaxes(a, 0, 1).reshape(t * b, c)


def _from_tm(a2d, n_t):
    rows, c = a2d.shape
    return jnp.swapaxes(a2d.reshape(n_t, rows // n_t, c), 0, 1)


def _rmsnorm(x, g):
    ms = jnp.mean(x * x, axis=-1, keepdims=True)
    return (x * lax.rsqrt(ms + _EPS)) * g


def _layernorm(x, g, b):
    mu = jnp.mean(x, axis=-1, keepdims=True)
    xc = x - mu
    y = xc * lax.rsqrt(jnp.mean(xc * xc, axis=-1, keepdims=True) + _EPS)
    return y * g + b


def _gelu(x):
    c = math.sqrt(2.0 / math.pi)
    return x * (0.5 * (1.0 + jnp.tanh(c * (x + 0.044715 * (x * x * x)))))


def _softplus(x):
    return jnp.maximum(x, 0.0) + jnp.log1p(jnp.exp(-jnp.abs(x)))


def _conv_rows(x, prev8, w, b):
    kw = w.shape[0]
    x0 = x[0:_SUBLANES]
    row = lax.broadcasted_iota(jnp.int32, x0.shape, 0)
    y = None
    y0 = None
    for k in range(kw):
        d = kw - 1 - k
        wk = w[k:k + 1]
        if d == 0:
            t, t0 = x * wk, x0 * wk
        else:
            t = pltpu.roll(x, d, 0) * wk
            t0 = jnp.where(row < d, pltpu.roll(prev8, d, 0), pltpu.roll(x0, d, 0)) * wk
        y = t if y is None else y + t
        y0 = t0 if y0 is None else y0 + t0
    return jnp.concatenate([y0, y[_SUBLANES:]], axis=0) + b


def _conv_tm(x, past, w, b):
    kw = w.shape[0]
    rows = x.shape[0]
    rb = past.shape[0] // (kw - 1)
    ext = jnp.concatenate([past, x], axis=0)
    y = ext[0:rows] * w[0:1]
    for k in range(1, kw):
        y = y + ext[k * rb:k * rb + rows] * w[k:k + 1]
    return y + b, ext[rows:]


def _sigmoid(z):
    return 0.5 * jnp.tanh(0.5 * z) + 0.5


def _lru_coeffs(xh, wa_half, ba_half, wx_half, bx_half, half_decay):
    xhb = xh.astype(_BF16)
    t_r = jnp.tanh(_mm(xhb, wa_half) + ba_half)
    t_i = jnp.tanh(_mm(xhb, wx_half) + bx_half)
    m = t_r * half_decay + half_decay
    a = jnp.exp(-m)
    u = jnp.tanh(m) * (1.0 + a * a)
    mult = jnp.where(u > 0.0, u * lax.rsqrt(u), 0.0)
    hx = 0.5 * xh
    return a, mult * (t_i * hx + hx)


def _scan_slabs(a_s, b_s, carry):
    rb = carry.shape[0]
    n_t = a_s.shape[0] // rb

    def body(t, h):
        rs = pl.ds(pl.multiple_of(t * rb, rb), rb)
        h = a_s[rs, :] * h + b_s[rs, :]
        b_s[rs, :] = h
        return h

    return lax.fori_loop(0, n_t, body, carry, unroll=min(n_t, _SUBLANES))


def _ffn_ple(layer, final, x_ref, p_ref, conv_fn, nf_ref, w_up_ref, cw_ref, cb_ref, w_down_ref,
             np_ref, w_gate_ref, w_proj_ref, nfin_ref, o_ref, acc_s):
    rows = x_ref.shape[0]
    d_ff = w_down_ref.shape[0]
    cn = min(_FFN_CHUNK, d_ff)
    rbk = min(_PLE_ROWS, rows)
    blocks = [slice(r * rbk, (r + 1) * rbk) for r in range(rows // rbk)]
    xn_blocks = [_rmsnorm(x_ref[rs, :], _vec(nf_ref, layer)).astype(_BF16) for rs in blocks]
    xn = jnp.concatenate(xn_blocks, axis=0)
    for j in range(d_ff // cn):
        cs = slice(j * cn, (j + 1) * cn)
        us = slice(d_ff + j * cn, d_ff + (j + 1) * cn)
        if j == 0:
            g = jnp.concatenate([_mm(xb, w_up_ref[:, cs]) for xb in xn_blocks], axis=0)
            u = jnp.concatenate([_mm(xb, w_up_ref[:, us]) for xb in xn_blocks], axis=0)
        else:
            g = _mm(xn, w_up_ref[:, cs])
            u = _mm(xn, w_up_ref[:, us])
        gc = conv_fn(cs, g, cw_ref[:, cs], cb_ref[layer:layer + 1, cs])
        d = _mm((_gelu(gc) * u).astype(_BF16), w_down_ref[cs, :])
        if j == 0:
            acc_s[...] = d
        else:
            acc_s[...] += d
    for rs in blocks:
        h = x_ref[rs, :] + acc_s[rs, :]
        gate = _sigmoid(_mm(_rmsnorm(h, _vec(np_ref, layer)).astype(_BF16), w_gate_ref[...]))
        h = h + gate * _mm(p_ref[rs, :].astype(_BF16), w_proj_ref[...])
        o_ref[rs, :] = _rmsnorm(h, nfin_ref[...]) if final else h


def _lru_kernel(layer, j, x_ref, h0_ref, past_ref, g_ref, w_in_ref, cw_ref, cb_ref, wa_ref, ba_ref,
                wx_ref, bx_ref, lam_ref, w_out_ref, o_ref, hlast_ref, tail_ref, tail_s, xb_s, gate_s, a_s, b_s):
    @pl.when(pl.program_id(0) == 0)
    def _():
        hlast_ref[...] = h0_ref[...]
        tail_s[...] = _to_tm(past_ref[...])

    heads, dh, _ = wa_ref.shape
    width = heads * dh
    bsz, tt, _ = x_ref.shape
    n_blk = _LRU_BLOCKS if tt % (_LRU_BLOCKS * _SUBLANES) == 0 else 1
    tb = tt // n_blk
    for r in range(n_blk):
        rs = slice(r * tb * bsz, (r + 1) * tb * bsz)
        x = _to_tm(x_ref[:, r * tb:(r + 1) * tb, :])
        xn = _rmsnorm(x, _vec(g_ref, layer)).astype(_BF16)
        xb_s[rs, :] = _mm(xn, w_in_ref[:, width:])
        gate_s[rs, :] = _mm(xn, w_in_ref[:, :width])
    half_decay = (0.5 * _LRU_C) * _softplus(-_vec(lam_ref, j))
    xc, tail = _conv_tm(xb_s[...], tail_s[...], cw_ref[...], _vec(cb_ref, j))
    tail_s[...] = tail
    tail_ref[...] = _from_tm(tail, tail_ref.shape[1])
    for hh in range(heads):
        sl = slice(hh * dh, (hh + 1) * dh)
        a, b = _lru_coeffs(xc[:, sl], wa_ref[hh], ba_ref[j:j + 1, sl], wx_ref[hh], bx_ref[j:j + 1, sl],
                           half_decay[:, sl])
        a_s[:, sl] = a
        b_s[:, sl] = b
    hlast_ref[...] = _scan_slabs(a_s, b_s, hlast_ref[...])
    for r in range(n_blk):
        rs = slice(r * tb * bsz, (r + 1) * tb * bsz)
        y = _mm((_gelu(gate_s[rs, :]) * b_s[rs, :]).astype(_BF16), w_out_ref[...])
        if len(o_ref.shape) == 3:
            ts = slice(r * tb, (r + 1) * tb)
            o_ref[:, ts, :] = x_ref[:, ts, :] + _from_tm(y, tb)
        else:
            o_ref[rs, :] = _to_tm(x_ref[:, r * tb:(r + 1) * tb, :]) + y


def _gmlp_seq_kernel(layer, j, x_ref, g_ref, w_in_ref, lng_ref, lnb_ref, ws_ref, bs_ref, w_out_ref,
                     o_ref, s_s):
    x = x_ref[...]
    rows = x.shape[0]
    groups, chunk, _ = ws_ref.shape
    width = s_s.shape[1]
    gd = width // groups
    xn = _rmsnorm(x, _vec(g_ref, layer)).astype(_BF16)
    n_piece = 2
    pw = width // n_piece
    zv = [_mm(xn, w_in_ref[:, width + k * pw:width + (k + 1) * pw]) for k in range(n_piece)]
    zu = [_mm(xn, w_in_ref[:, k * pw:(k + 1) * pw]) for k in range(n_piece)]
    v = jnp.concatenate([_gelu(z) for z in zv], axis=1)
    v = _layernorm(v, _vec(lng_ref, j), _vec(lnb_ref, j)).astype(_BF16)
    tri = (lax.broadcasted_iota(jnp.int32, (chunk, chunk), 0)
           >= lax.broadcasted_iota(jnp.int32, (chunk, chunk), 1))
    n_c = rows // chunk
    for g in range(groups):
        wg = jnp.where(tri, ws_ref[g], 0.0).astype(_BF16)
        cs = slice(g * gd, (g + 1) * gd)
        vg = jnp.concatenate([v[c * chunk:(c + 1) * chunk, cs] for c in range(n_c)], axis=1)
        sg = _mm(wg, vg)
        for c in range(n_c):
            s_s[c * chunk:(c + 1) * chunk, cs] = sg[:, c * gd:(c + 1) * gd] + bs_ref[:, cs]
    u = jnp.concatenate([_gelu(z) for z in zu], axis=1)
    o_ref[...] = x + _mm((u * s_s[...]).astype(_BF16), w_out_ref[...])


def _ffn_seq_kernel(layer, final, x_ref, p_ref, nf_ref, w_up_ref, cw_ref, cb_ref, w_down_ref,
                    np_ref, w_gate_ref, w_proj_ref, nfin_ref, o_ref, tail_ref, prev_s, acc_s):
    @pl.when(pl.program_id(1) == 0)
    def _():
        prev_s[...] = jnp.zeros_like(prev_s)

    rows = x_ref.shape[0]
    keep = tail_ref.shape[0]

    def conv(cs, g, w, b):
        gc = _conv_rows(g, prev_s[:, cs], w, b)
        prev_s[:, cs] = g[rows - _SUBLANES:rows, :]
        tail_ref[:, cs] = g[rows - keep:rows, :]
        return gc

    _ffn_ple(layer, final, x_ref, p_ref, conv, nf_ref, w_up_ref, cw_ref, cb_ref, w_down_ref,
             np_ref, w_gate_ref, w_proj_ref, nfin_ref, o_ref, acc_s)


def _gmlp_tm_kernel(layer, j, n_t, x_ref, g_ref, w_in_ref, lng_ref, lnb_ref, wsc_ref, bsc_ref, w_out_ref,
                    o_ref, v_ref, s_s):
    x = x_ref[...]
    width = s_s.shape[1]
    bs = x.shape[0] // n_t
    z = _gelu(_mm(_rmsnorm(x, _vec(g_ref, layer)).astype(_BF16), w_in_ref[...]))
    u = z[:, :width]
    v = _layernorm(z[:, width:], _vec(lng_ref, j), _vec(lnb_ref, j))
    v_ref[...] = _from_tm(v, n_t)
    for t in range(n_t):
        s = bsc_ref[t:t + 1, :]
        for q in range(t + 1):
            s = s + wsc_ref[t, q:q + 1, :] * v[q * bs:(q + 1) * bs, :]
        s_s[t * bs:(t + 1) * bs, :] = s
    o_ref[...] = x + _mm((u * s_s[...]).astype(_BF16), w_out_ref[...])


def _ffn_tm_kernel(layer, final, x_ref, p_ref, past_ref, nf_ref, w_up_ref, cw_ref, cb_ref, w_down_ref,
                   np_ref, w_gate_ref, w_proj_ref, nfin_ref, o_ref, tail_ref, p_s, past_s, tail_s, o_s, acc_s):
    n_keep = past_ref.shape[1]
    p_s[...] = _to_tm(p_ref[...])
    past_s[...] = _to_tm(past_ref[...])

    def conv(cs, g, w, b):
        gc, tail = _conv_tm(g, past_s[:, cs], w, b)
        tail_s[:, cs] = tail
        return gc

    _ffn_ple(layer, final, x_ref, p_s, conv, nf_ref, w_up_ref, cw_ref, cb_ref, w_down_ref,
             np_ref, w_gate_ref, w_proj_ref, nfin_ref, o_s if final else o_ref, acc_s)
    tail_ref[...] = _from_tm(tail_s[...], n_keep)
    if final:
        o_ref[...] = _from_tm(o_s[...], o_ref.shape[1])


def _whole(a):
    zeros = (0,) * a.ndim
    return pl.BlockSpec(a.shape, lambda *_: zeros, pipeline_mode=pl.Buffered(1))


def _layer_of(a, i):
    zeros = (0,) * (a.ndim - 1)
    return pl.BlockSpec((None,) + a.shape[1:], lambda *_: (i,) + zeros, pipeline_mode=pl.Buffered(1))


def _params(n_axes):
    return pltpu.CompilerParams(dimension_semantics=("arbitrary",) * n_axes, vmem_limit_bytes=_VMEM_LIMIT)


def _row_spec(tm, d):
    return pl.BlockSpec((None, tm, d), lambda b, t: (b, t, 0))


def _state_spec(shape):
    zeros = (0,) * len(shape)
    return pl.BlockSpec((None,) + shape, lambda b, t: (b,) + zeros)


def _full_spec(shape):
    zeros = (0,) * len(shape)
    return pl.BlockSpec(shape, lambda *_: zeros)


def _lru_specs(w, j):
    return [_whole(w["norm_mix"]), _layer_of(w["lru_w_in"], j), _layer_of(w["lru_conv_w"], j),
            _whole(w["lru_conv_b"]), _layer_of(w["lru_w_a"], j), _whole(w["lru_b_a"]),
            _layer_of(w["lru_w_x"], j), _whole(w["lru_b_x"]), _whole(w["lru_lambda"]),
            _layer_of(w["lru_w_out"], j)]


def _lru_args(w):
    return [w[k] for k in ("norm_mix", "lru_w_in", "lru_conv_w", "lru_conv_b", "lru_w_a", "lru_b_a",
                           "lru_w_x", "lru_b_x", "lru_lambda", "lru_w_out")]


def _ffn_specs(w, i):
    return [_whole(w["norm_ffn"]), _layer_of(w["ffn_w_up"], i), _layer_of(w["ffn_conv_w"], i),
            _whole(w["ffn_conv_b"]), _layer_of(w["ffn_w_down"], i), _whole(w["norm_ple"]),
            _layer_of(w["ple_w_gate"], i), _layer_of(w["ple_w_proj"], i), _whole(w["norm_final"])]


def _ffn_args(w):
    return [w[k] for k in ("norm_ffn", "ffn_w_up", "ffn_conv_w", "ffn_conv_b", "ffn_w_down", "norm_ple",
                           "ple_w_gate", "ple_w_proj", "norm_final")]


def _lru(x, h0, past, w, layer, j, prompt):
    bsz, t, d = x.shape
    width = w["lru_w_out"].shape[1]
    if prompt:
        tt = min(_LRU_TILE // bsz, t)
        assert tt % _SUBLANES == 0 and t % tt == 0 and bsz % _SUBLANES == 0
        o_shape, o_spec = x.shape, pl.BlockSpec((bsz, tt, d), lambda t: (0, t, 0))
    else:
        tt = t
        o_shape, o_spec = (t * bsz, d), _full_spec((t * bsz, d))
    out_shape = [jax.ShapeDtypeStruct(o_shape, _F32), jax.ShapeDtypeStruct(h0.shape, _F32),
                 jax.ShapeDtypeStruct(past.shape, _F32)]
    return pl.pallas_call(
        functools.partial(_lru_kernel, layer, j),
        grid=(t // tt,),
        in_specs=[pl.BlockSpec((bsz, tt, d), lambda t: (0, t, 0)), _whole(h0), _whole(past)]
        + _lru_specs(w, j),
        out_specs=[o_spec, _full_spec(h0.shape), _full_spec(past.shape)],
        out_shape=out_shape,
        scratch_shapes=[pltpu.VMEM((past.shape[1] * bsz, width), _F32)]
        + [pltpu.VMEM((bsz * tt, width), _F32)] * 4,
        compiler_params=_params(1),
        name="lru_seq" if prompt else "lru_tm",
    )(x, h0, past, *_lru_args(w))


def _gmlp_seq(x, w, layer, j):
    bsz, t, d = x.shape
    width = w["gm_w_out"].shape[1]
    tm = min(_GMLP_TILE, t)
    return pl.pallas_call(
        functools.partial(_gmlp_seq_kernel, layer, j),
        grid=(bsz, t // tm),
        in_specs=[_row_spec(tm, d), _whole(w["norm_mix"]), _layer_of(w["gm_w_in"], j), _whole(w["gm_ln_g"]),
                  _whole(w["gm_ln_b"]), _layer_of(w["gm_w_s"], j), _layer_of(w["gm_b_s_rows"], j),
                  _layer_of(w["gm_w_out"], j)],
        out_specs=_row_spec(tm, d),
        out_shape=jax.ShapeDtypeStruct((bsz, t, d), _F32),
        scratch_shapes=[pltpu.VMEM((tm, width), _F32)],
        compiler_params=_params(2),
        name="gmlp_seq",
    )(x, w["norm_mix"], w["gm_w_in"], w["gm_ln_g"], w["gm_ln_b"], w["gm_w_s"], w["gm_b_s_rows"],
      w["gm_w_out"])


def _ffn_seq(x, p_all, w, layer, final, n_keep):
    bsz, t, d = x.shape
    pdim = p_all.shape[-1]
    d_ff = w["ffn_w_down"].shape[1]
    tm = min(_FFN_TILE, t)
    p_spec = pl.BlockSpec((None, None, tm, pdim), lambda b, t: (layer, b, t, 0))
    return pl.pallas_call(
        functools.partial(_ffn_seq_kernel, layer, final),
        grid=(bsz, t // tm),
        in_specs=[_row_spec(tm, d), p_spec] + _ffn_specs(w, layer),
        out_specs=[_row_spec(tm, d), _state_spec((n_keep, d_ff))],
        out_shape=[jax.ShapeDtypeStruct((bsz, t, d), _F32),
                   jax.ShapeDtypeStruct((bsz, n_keep, d_ff), _F32)],
        scratch_shapes=[pltpu.VMEM((_SUBLANES, d_ff), _F32), pltpu.VMEM((tm, d), _F32)],
        compiler_params=_params(2),
        name="ffn_seq",
    )(x, p_all, *_ffn_args(w))


def _gmlp_tm(x2d, n_t, w, layer, j):
    rows, d = x2d.shape
    width = w["gm_w_out"].shape[1]
    out_shape = [jax.ShapeDtypeStruct((rows, d), _F32), jax.ShapeDtypeStruct((rows // n_t, n_t, width), _F32)]
    return pl.pallas_call(
        functools.partial(_gmlp_tm_kernel, layer, j, n_t),
        grid=(1,),
        in_specs=[_full_spec(x2d.shape), _whole(w["norm_mix"]), _layer_of(w["gm_w_in"], j), _whole(w["gm_ln_g"]),
                  _whole(w["gm_ln_b"]), _layer_of(w["gm_w_s_head"], j), _layer_of(w["gm_b_s_head"], j),
                  _layer_of(w["gm_w_out"], j)],
        out_specs=[_full_spec(s.shape) for s in out_shape],
        out_shape=out_shape,
        scratch_shapes=[pltpu.VMEM((rows, width), _F32)],
        compiler_params=_params(1),
        name="gmlp_tm",
    )(x2d, w["norm_mix"], w["gm_w_in"], w["gm_ln_g"], w["gm_ln_b"], w["gm_w_s_head"], w["gm_b_s_head"],
      w["gm_w_out"])


def _ffn_tm(x2d, p_all, past_all, w, layer, final):
    rows, d = x2d.shape
    _, bsz, n_t, pdim = p_all.shape
    n_keep, d_ff = past_all.shape[2], past_all.shape[3]
    o_shape = (bsz, n_t, d) if final else (rows, d)
    out_shape = [jax.ShapeDtypeStruct(o_shape, _F32), jax.ShapeDtypeStruct(past_all.shape[1:], _F32)]
    return pl.pallas_call(
        functools.partial(_ffn_tm_kernel, layer, final),
        grid=(1,),
        in_specs=[_full_spec(x2d.shape), _layer_of(p_all, layer), _layer_of(past_all, layer)]
        + _ffn_specs(w, layer),
        out_specs=[_full_spec(s.shape) for s in out_shape],
        out_shape=out_shape,
        scratch_shapes=[pltpu.VMEM((rows, pdim), _F32), pltpu.VMEM((n_keep * bsz, d_ff), _F32),
                        pltpu.VMEM((n_keep * bsz, d_ff), _F32), pltpu.VMEM((rows, d), _F32),
                        pltpu.VMEM((rows, d), _F32)],
        compiler_params=_params(1),
        name="ffn_tm",
    )(x2d, p_all, past_all, *_ffn_args(w))


def kernel(x_prompt, x_sample, p_prompt, p_sample, state_lru_h, state_lru_conv, state_ffn_conv, norm_mix, norm_ffn, norm_ple, norm_final, lru_w_in, lru_conv_w, lru_conv_b, lru_w_a, lru_b_a, lru_w_x, lru_b_x, lru_lambda, lru_w_out, gm_w_in, gm_ln_g, gm_ln_b, gm_w_s, gm_b_s, gm_w_out, ffn_w_up, ffn_conv_w, ffn_conv_b, ffn_w_down, ple_w_gate, ple_w_proj):
    depth = norm_mix.shape[0]
    bsz, seq, d_model = x_prompt.shape
    dec_b, dec_t, _ = x_sample.shape
    d_ff = ffn_w_down.shape[1]
    groups, chunk = gm_w_s.shape[1], gm_w_s.shape[2]
    gd = gm_w_out.shape[1] // groups
    tm = min(_GMLP_TILE, seq)
    assert seq % tm == 0 and tm % chunk == 0 and tm % _SUBLANES == 0 and d_ff % min(_FFN_CHUNK, d_ff) == 0
    assert _PAST_LEN % chunk == 0 and dec_t <= chunk and dec_b % _SUBLANES == 0
    assert seq % min(_FFN_TILE, seq) == 0 and depth % 2 == 0

    w = dict(
        norm_mix=norm_mix, norm_ffn=norm_ffn, norm_ple=norm_ple, norm_final=norm_final.reshape(1, -1),
        lru_w_in=lru_w_in.astype(_BF16), lru_conv_w=lru_conv_w, lru_conv_b=lru_conv_b,
        lru_w_a=(0.5 * lru_w_a).astype(_BF16), lru_b_a=0.5 * lru_b_a,
        lru_w_x=(0.5 * lru_w_x).astype(_BF16), lru_b_x=0.5 * lru_b_x,
        lru_lambda=lru_lambda, lru_w_out=lru_w_out.astype(_BF16),
        gm_w_in=gm_w_in.astype(_BF16), gm_ln_g=gm_ln_g, gm_ln_b=gm_ln_b, gm_w_s=gm_w_s,
        gm_w_out=gm_w_out.astype(_BF16),
        gm_b_s_rows=jnp.repeat(jnp.transpose(gm_b_s, (0, 2, 1)), gd, axis=2),
        gm_w_s_head=jnp.repeat(jnp.transpose(gm_w_s[:, :, :dec_t, :dec_t], (0, 2, 3, 1)), gd, axis=3),
        gm_b_s_head=jnp.repeat(jnp.transpose(gm_b_s[:, :, :dec_t], (0, 2, 1)), gd, axis=2),
        ffn_w_up=ffn_w_up.astype(_BF16), ffn_conv_w=ffn_conv_w, ffn_conv_b=ffn_conv_b,
        ffn_w_down=ffn_w_down.astype(_BF16), ple_w_gate=ple_w_gate.astype(_BF16),
        ple_w_proj=ple_w_proj.astype(_BF16),
    )

    hp = x_prompt
    hs = x_sample
    lru_h_p, lru_conv_p, ffn_conv_p = [], [], []
    lru_h_s, lru_conv_s, ffn_conv_s, gm_v_s = [], [], [], []
    for i in range(depth):
        j = i // 2
        if i % 2 == 0:
            kw = lru_conv_w.shape[1]
            width = lru_w_out.shape[1]
            hp, hlast, tail = _lru(hp, jnp.zeros((bsz, width), _F32), jnp.zeros((bsz, kw - 1, width), _F32),
                                   w, i, j, True)
            lru_h_p.append(hlast)
            lru_conv_p.append(tail)
            if hs.ndim == 2:
                hs = _from_tm(hs, dec_t)
            hs, hlast, tail = _lru(hs, state_lru_h[j], state_lru_conv[j], w, i, j, False)
            lru_h_s.append(hlast)
            lru_conv_s.append(tail)
        else:
            hp = _gmlp_seq(hp, w, i, j)
            hs, v = _gmlp_tm(hs, dec_t, w, i, j)
            gm_v_s.append(v)
        kw = ffn_conv_w.shape[1]
        final = i == depth - 1
        hp, tail = _ffn_seq(hp, p_prompt, w, i, final, kw - 1)
        ffn_conv_p.append(tail)
        hs, tail = _ffn_tm(hs, p_sample, state_ffn_conv, w, i, final)
        ffn_conv_s.append(tail)

    def stack(xs):
        return xs[0][None] if len(xs) == 1 else jnp.stack(xs)

    return (hp, hs, stack(lru_h_p), stack(lru_conv_p), stack(ffn_conv_p),
            stack(lru_h_s), stack(lru_conv_s), stack(ffn_conv_s), stack(gm_v_s))
```

```python
import functools
import math

import jax
import jax.numpy as jnp
from jax import lax
from jax.experimental import pallas as pl
from jax.experimental.pallas import tpu as pltpu

_EPS = 1e-6
_LRU_C = 8.0
_PAST_LEN = 16384
_SUBLANES = 8
_LRU_TILE = 1024
_GMLP_TILE = 1024
_FFN_TILE = 1024
_FFN_CHUNK = 1024
_PLE_ROWS = 256
_LRU_BLOCKS = 4
_FFN_TM_CHUNK = 512
_VMEM_LIMIT = 56 * 1024 * 1024

_F32 = jnp.float32
_BF16 = jnp.bfloat16


def _mm(a, b):
    return jnp.dot(a, b, preferred_element_type=_F32)


def _vec(ref, i):
    return ref[i:i + 1, :]


def _to_tm(a):
    b, t, c = a.shape
    return jnp.swapaxes(a, 0, 1).reshape(t * b, c)


def _from_tm(a2d, n_t):
    rows, c = a2d.shape
    return jnp.swapaxes(a2d.reshape(n_t, rows // n_t, c), 0, 1)


def _rmsnorm(x, g):
    ms = jnp.mean(x * x, axis=-1, keepdims=True)
    return (x * lax.rsqrt(ms + _EPS)) * g


def _layernorm(x, g, b):
    mu = jnp.mean(x, axis=-1, keepdims=True)
    xc = x - mu
    y = xc * lax.rsqrt(jnp.mean(xc * xc, axis=-1, keepdims=True) + _EPS)
    return y * g + b


def _gelu(x):
    c = math.sqrt(2.0 / math.pi)
    return x * (0.5 * (1.0 + jnp.tanh(c * (x + 0.044715 * (x * x * x)))))


def _softplus(x):
    return jnp.maximum(x, 0.0) + jnp.log1p(jnp.exp(-jnp.abs(x)))


def _conv_rows(x, prev8, w, b):
    kw = w.shape[0]
    x0 = x[0:_SUBLANES]
    row = lax.broadcasted_iota(jnp.int32, x0.shape, 0)
    y = None
    y0 = None
    for k in range(kw):
        d = kw - 1 - k
        wk = w[k:k + 1]
        if d == 0:
            t, t0 = x * wk, x0 * wk
        else:
            t = pltpu.roll(x, d, 0) * wk
            t0 = jnp.where(row < d, pltpu.roll(prev8, d, 0), pltpu.roll(x0, d, 0)) * wk
        y = t if y is None else y + t
        y0 = t0 if y0 is None else y0 + t0
    return jnp.concatenate([y0, y[_SUBLANES:]], axis=0) + b


def _conv_tm(x, past, w, b):
    kw = w.shape[0]
    rows = x.shape[0]
    rb = past.shape[0] // (kw - 1)
    ext = jnp.concatenate([past, x], axis=0)
    y = ext[0:rows] * w[0:1]
    for k in range(1, kw):
        y = y + ext[k * rb:k * rb + rows] * w[k:k + 1]
    return y + b, ext[rows:]


def _sigmoid(z):
    return 0.5 * jnp.tanh(0.5 * z) + 0.5


def _lru_coeffs(xh, wa_half, ba_half, wx_half, bx_half, half_decay):
    xhb = xh.astype(_BF16)
    t_r = jnp.tanh(_mm(xhb, wa_half) + ba_half)
    t_i = jnp.tanh(_mm(xhb, wx_half) + bx_half)
    m = t_r * half_decay + half_decay
    a = jnp.exp(-m)
    u = jnp.tanh(m) * (1.0 + a * a)
    mult = jnp.where(u > 0.0, u * lax.rsqrt(u), 0.0)
    hx = 0.5 * xh
    return a, mult * (t_i * hx + hx)


def _scan_slabs(a_s, b_s, carry):
    rb = carry.shape[0]
    n_t = a_s.shape[0] // rb

    def body(t, h):
        rs = pl.ds(pl.multiple_of(t * rb, rb), rb)
        h = a_s[rs, :] * h + b_s[rs, :]
        b_s[rs, :] = h
        return h

    return lax.fori_loop(0, n_t, body, carry, unroll=min(n_t, _SUBLANES))


def _ffn_ple(layer, final, x_ref, p_ref, conv_fn, nf_ref, wg_ref, wu_ref, cw_ref, cb_ref, w_down_ref,
             np_ref, w_gate_ref, w_proj_ref, nfin_ref, o_ref, acc_s):
    rows = x_ref.shape[0]
    d_ff = w_down_ref.shape[0]
    cn = min(_FFN_CHUNK, d_ff)
    rbk = min(_PLE_ROWS, rows)
    blocks = [slice(r * rbk, (r + 1) * rbk) for r in range(rows // rbk)]
    xn_blocks = [_rmsnorm(x_ref[rs, :], _vec(nf_ref, layer)).astype(_BF16) for rs in blocks]
    xn = jnp.concatenate(xn_blocks, axis=0)
    for j in range(d_ff // cn):
        cs = slice(j * cn, (j + 1) * cn)
        if j == 0:
            g = jnp.concatenate([_mm(xb, wg_ref[:, cs]) for xb in xn_blocks], axis=0)
            u = jnp.concatenate([_mm(xb, wu_ref[:, cs]) for xb in xn_blocks], axis=0)
        else:
            g = _mm(xn, wg_ref[:, cs])
            u = _mm(xn, wu_ref[:, cs])
        gc = conv_fn(cs, g, cw_ref[:, cs], cb_ref[layer:layer + 1, cs])
        d = _mm((_gelu(gc) * u).astype(_BF16), w_down_ref[cs, :])
        if j == 0:
            acc_s[...] = d
        else:
            acc_s[...] += d
    for rs in blocks:
        h = x_ref[rs, :] + acc_s[rs, :]
        gate = _sigmoid(_mm(_rmsnorm(h, _vec(np_ref, layer)).astype(_BF16), w_gate_ref[...]))
        h = h + gate * _mm(p_ref[rs, :].astype(_BF16), w_proj_ref[...])
        o_ref[rs, :] = _rmsnorm(h, nfin_ref[...]) if final else h


def _lru_kernel(layer, j, x_ref, h0_ref, past_ref, g_ref, w_in_ref, cw_ref, cb_ref, wa_ref, ba_ref,
                wx_ref, bx_ref, lam_ref, w_out_ref, o_ref, hlast_ref, tail_ref, tail_s, xb_s, gate_s, a_s, b_s):
    @pl.when(pl.program_id(0) == 0)
    def _():
        hlast_ref[...] = h0_ref[...]
        tail_s[...] = _to_tm(past_ref[...])

    heads, dh, _ = wa_ref.shape
    width = heads * dh
    bsz, tt, _ = x_ref.shape
    n_blk = _LRU_BLOCKS if tt % (_LRU_BLOCKS * _SUBLANES) == 0 else 1
    tb = tt // n_blk
    for r in range(n_blk):
        rs = slice(r * tb * bsz, (r + 1) * tb * bsz)
        x = _to_tm(x_ref[:, r * tb:(r + 1) * tb, :])
        xn = _rmsnorm(x, _vec(g_ref, layer)).astype(_BF16)
        xb_s[rs, :] = _mm(xn, w_in_ref[:, width:])
        gate_s[rs, :] = _mm(xn, w_in_ref[:, :width])
    half_decay = (0.5 * _LRU_C) * _softplus(-_vec(lam_ref, j))
    xc, tail = _conv_tm(xb_s[...], tail_s[...], cw_ref[...], _vec(cb_ref, j))
    tail_s[...] = tail
    tail_ref[...] = _from_tm(tail, tail_ref.shape[1])
    for hh in range(heads):
        sl = slice(hh * dh, (hh + 1) * dh)
        a, b = _lru_coeffs(xc[:, sl], wa_ref[hh], ba_ref[j:j + 1, sl], wx_ref[hh], bx_ref[j:j + 1, sl],
                           half_decay[:, sl])
        a_s[:, sl] = a
        b_s[:, sl] = b
    hlast_ref[...] = _scan_slabs(a_s, b_s, hlast_ref[...])
    for r in range(n_blk):
        rs = slice(r * tb * bsz, (r + 1) * tb * bsz)
        y = _mm((_gelu(gate_s[rs, :]) * b_s[rs, :]).astype(_BF16), w_out_ref[...])
        if len(o_ref.shape) == 3:
            ts = slice(r * tb, (r + 1) * tb)
            o_ref[:, ts, :] = x_ref[:, ts, :] + _from_tm(y, tb)
        else:
            o_ref[rs, :] = _to_tm(x_ref[:, r * tb:(r + 1) * tb, :]) + y


def _gmlp_seq_kernel(layer, j, x_ref, g_ref, w_in_ref, lng_ref, lnb_ref, ws_ref, bs_ref, w_out_ref,
                     o_ref, s_s):
    x = x_ref[...]
    rows = x.shape[0]
    groups, chunk, _ = ws_ref.shape
    width = s_s.shape[1]
    gd = width // groups
    xn = _rmsnorm(x, _vec(g_ref, layer)).astype(_BF16)
    n_piece = 2
    pw = width // n_piece
    zv = [_mm(xn, w_in_ref[:, width + k * pw:width + (k + 1) * pw]) for k in range(n_piece)]
    zu = [_mm(xn, w_in_ref[:, k * pw:(k + 1) * pw]) for k in range(n_piece)]
    v = jnp.concatenate([_gelu(z) for z in zv], axis=1)
    v = _layernorm(v, _vec(lng_ref, j), _vec(lnb_ref, j)).astype(_BF16)
    tri = (lax.broadcasted_iota(jnp.int32, (chunk, chunk), 0)
           >= lax.broadcasted_iota(jnp.int32, (chunk, chunk), 1))
    n_c = rows // chunk
    for g in range(groups):
        wg = jnp.where(tri, ws_ref[g], 0.0).astype(_BF16)
        cs = slice(g * gd, (g + 1) * gd)
        vg = jnp.concatenate([v[c * chunk:(c + 1) * chunk, cs] for c in range(n_c)], axis=1)
        sg = _mm(wg, vg)
        for c in range(n_c):
            s_s[c * chunk:(c + 1) * chunk, cs] = sg[:, c * gd:(c + 1) * gd] + bs_ref[:, cs]
    u = jnp.concatenate([_gelu(z) for z in zu], axis=1)
    o_ref[...] = x + _mm((u * s_s[...]).astype(_BF16), w_out_ref[...])


def _ffn_seq_kernel(layer, final, x_ref, p_ref, nf_ref, wg_ref, wu_ref, cw_ref, cb_ref, w_down_ref,
                    np_ref, w_gate_ref, w_proj_ref, nfin_ref, o_ref, tail_ref, prev_s, acc_s):
    @pl.when(pl.program_id(1) == 0)
    def _():
        prev_s[...] = jnp.zeros_like(prev_s)

    rows = x_ref.shape[0]
    keep = tail_ref.shape[0]

    def conv(cs, g, w, b):
        gc = _conv_rows(g, prev_s[:, cs], w, b)
        prev_s[:, cs] = g[rows - _SUBLANES:rows, :]
        tail_ref[:, cs] = g[rows - keep:rows, :]
        return gc

    _ffn_ple(layer, final, x_ref, p_ref, conv, nf_ref, wg_ref, wu_ref, cw_ref, cb_ref, w_down_ref,
             np_ref, w_gate_ref, w_proj_ref, nfin_ref, o_ref, acc_s)


def _gmlp_tm_kernel(layer, j, n_t, x_ref, g_ref, w_in_ref, lng_ref, lnb_ref, wsc_ref, bsc_ref, w_out_ref,
                    o_ref, v_ref, s_s):
    x = x_ref[...]
    width = s_s.shape[1]
    bs = x.shape[0] // n_t
    z = _gelu(_mm(_rmsnorm(x, _vec(g_ref, layer)).astype(_BF16), w_in_ref[...]))
    u = z[:, :width]
    v = _layernorm(z[:, width:], _vec(lng_ref, j), _vec(lnb_ref, j))
    v_ref[...] = _from_tm(v, n_t)
    for t in range(n_t):
        s = bsc_ref[t:t + 1, :]
        for q in range(t + 1):
            s = s + wsc_ref[t, q:q + 1, :] * v[q * bs:(q + 1) * bs, :]
        s_s[t * bs:(t + 1) * bs, :] = s
    o_ref[...] = x + _mm((u * s_s[...]).astype(_BF16), w_out_ref[...])


def _ffn_tm_kernel(layer, final, x_ref, p_ref, past_ref, nf_ref, wg_ref, wu_ref, cw_ref, cb_ref, wd_ref,
                   np_ref, w_gate_ref, w_proj_ref, nfin_ref, o_ref, tail_ref, wg_bf_ref, wu_bf_ref,
                   wd_bf_ref, gate_bf_ref, proj_bf_ref, xn_s, acc_s):
    j = pl.program_id(0)
    n_keep = past_ref.shape[1]

    @pl.when(j == 0)
    def _():
        xn_s[...] = _rmsnorm(x_ref[...], _vec(nf_ref, layer)).astype(_BF16)
        acc_s[...] = jnp.zeros_like(acc_s)

    wg = wg_ref[...].astype(_BF16)
    wu = wu_ref[...].astype(_BF16)
    wd = wd_ref[...].astype(_BF16)
    wg_bf_ref[...] = wg
    wu_bf_ref[...] = wu
    wd_bf_ref[...] = wd
    xn = xn_s[...]
    g = _mm(xn, wg)
    u = _mm(xn, wu)
    gc, tail = _conv_tm(g, _to_tm(past_ref[...]), cw_ref[...], cb_ref[layer:layer + 1, :])
    tail_ref[...] = _from_tm(tail, n_keep)
    acc_s[...] += _mm((_gelu(gc) * u).astype(_BF16), wd)

    @pl.when(j == pl.num_programs(0) - 1)
    def _():
        w_gate = w_gate_ref[...].astype(_BF16)
        w_proj = w_proj_ref[...].astype(_BF16)
        gate_bf_ref[...] = w_gate
        proj_bf_ref[...] = w_proj
        h = x_ref[...] + acc_s[...]
        gate = _sigmoid(_mm(_rmsnorm(h, _vec(np_ref, layer)).astype(_BF16), w_gate))
        h = h + gate * _mm(_to_tm(p_ref[...]).astype(_BF16), w_proj)
        if final:
            o_ref[...] = _from_tm(_rmsnorm(h, nfin_ref[...]), o_ref.shape[1])
        else:
            o_ref[...] = h


def _whole(a):
    zeros = (0,) * a.ndim
    return pl.BlockSpec(a.shape, lambda *_: zeros, pipeline_mode=pl.Buffered(1))


def _layer_of(a, i):
    zeros = (0,) * (a.ndim - 1)
    return pl.BlockSpec((None,) + a.shape[1:], lambda *_: (i,) + zeros, pipeline_mode=pl.Buffered(1))


def _params(n_axes):
    return pltpu.CompilerParams(dimension_semantics=("arbitrary",) * n_axes, vmem_limit_bytes=_VMEM_LIMIT)


def _row_spec(tm, d):
    return pl.BlockSpec((None, tm, d), lambda b, t: (b, t, 0))


def _state_spec(shape):
    zeros = (0,) * len(shape)
    return pl.BlockSpec((None,) + shape, lambda b, t: (b,) + zeros)


def _full_spec(shape):
    zeros = (0,) * len(shape)
    return pl.BlockSpec(shape, lambda *_: zeros)


def _lru_specs(w, j):
    return [_whole(w["norm_mix"]), _layer_of(w["lru_w_in"], j), _layer_of(w["lru_conv_w"], j),
            _whole(w["lru_conv_b"]), _layer_of(w["lru_w_a"], j), _whole(w["lru_b_a"]),
            _layer_of(w["lru_w_x"], j), _whole(w["lru_b_x"]), _whole(w["lru_lambda"]),
            _layer_of(w["lru_w_out"], j)]


def _lru_args(w):
    return [w[k] for k in ("norm_mix", "lru_w_in", "lru_conv_w", "lru_conv_b", "lru_w_a", "lru_b_a",
                           "lru_w_x", "lru_b_x", "lru_lambda", "lru_w_out")]


def _ffn_specs(w, i):
    wg, wu, wd, w_gate, w_proj = w["ffn_bf16"][i]
    return [_whole(w["norm_ffn"]), _whole(wg), _whole(wu), _layer_of(w["ffn_conv_w"], i),
            _whole(w["ffn_conv_b"]), _whole(wd), _whole(w["norm_ple"]), _whole(w_gate), _whole(w_proj),
            _whole(w["norm_final"])]


def _ffn_args(w, i):
    wg, wu, wd, w_gate, w_proj = w["ffn_bf16"][i]
    return [w["norm_ffn"], wg, wu, w["ffn_conv_w"], w["ffn_conv_b"], wd, w["norm_ple"], w_gate, w_proj,
            w["norm_final"]]


def _lru(x, h0, past, w, layer, j, prompt):
    bsz, t, d = x.shape
    width = w["lru_w_out"].shape[1]
    if prompt:
        tt = min(_LRU_TILE // bsz, t)
        assert tt % _SUBLANES == 0 and t % tt == 0 and bsz % _SUBLANES == 0
        o_shape, o_spec = x.shape, pl.BlockSpec((bsz, tt, d), lambda t: (0, t, 0))
    else:
        tt = t
        o_shape, o_spec = (t * bsz, d), _full_spec((t * bsz, d))
    out_shape = [jax.ShapeDtypeStruct(o_shape, _F32), jax.ShapeDtypeStruct(h0.shape, _F32),
                 jax.ShapeDtypeStruct(past.shape, _F32)]
    return pl.pallas_call(
        functools.partial(_lru_kernel, layer, j),
        grid=(t // tt,),
        in_specs=[pl.BlockSpec((bsz, tt, d), lambda t: (0, t, 0)), _whole(h0), _whole(past)]
        + _lru_specs(w, j),
        out_specs=[o_spec, _full_spec(h0.shape), _full_spec(past.shape)],
        out_shape=out_shape,
        scratch_shapes=[pltpu.VMEM((past.shape[1] * bsz, width), _F32)]
        + [pltpu.VMEM((bsz * tt, width), _F32)] * 4,
        compiler_params=_params(1),
        name="lru_seq" if prompt else "lru_tm",
    )(x, h0, past, *_lru_args(w))


def _gmlp_seq(x, w, layer, j):
    bsz, t, d = x.shape
    width = w["gm_w_out"].shape[1]
    tm = min(_GMLP_TILE, t)
    return pl.pallas_call(
        functools.partial(_gmlp_seq_kernel, layer, j),
        grid=(bsz, t // tm),
        in_specs=[_row_spec(tm, d), _whole(w["norm_mix"]), _layer_of(w["gm_w_in"], j), _whole(w["gm_ln_g"]),
                  _whole(w["gm_ln_b"]), _layer_of(w["gm_w_s"], j), _layer_of(w["gm_b_s_rows"], j),
                  _layer_of(w["gm_w_out"], j)],
        out_specs=_row_spec(tm, d),
        out_shape=jax.ShapeDtypeStruct((bsz, t, d), _F32),
        scratch_shapes=[pltpu.VMEM((tm, width), _F32)],
        compiler_params=_params(2),
        name="gmlp_seq",
    )(x, w["norm_mix"], w["gm_w_in"], w["gm_ln_g"], w["gm_ln_b"], w["gm_w_s"], w["gm_b_s_rows"],
      w["gm_w_out"])


def _ffn_seq(x, p_all, w, layer, final, n_keep):
    bsz, t, d = x.shape
    pdim = p_all.shape[-1]
    d_ff = w["ffn_conv_w"].shape[2]
    tm = min(_FFN_TILE, t)
    p_spec = pl.BlockSpec((None, None, tm, pdim), lambda b, t: (layer, b, t, 0))
    return pl.pallas_call(
        functools.partial(_ffn_seq_kernel, layer, final),
        grid=(bsz, t // tm),
        in_specs=[_row_spec(tm, d), p_spec] + _ffn_specs(w, layer),
        out_specs=[_row_spec(tm, d), _state_spec((n_keep, d_ff))],
        out_shape=[jax.ShapeDtypeStruct((bsz, t, d), _F32),
                   jax.ShapeDtypeStruct((bsz, n_keep, d_ff), _F32)],
        scratch_shapes=[pltpu.VMEM((_SUBLANES, d_ff), _F32), pltpu.VMEM((tm, d), _F32)],
        compiler_params=_params(2),
        name="ffn_seq",
    )(x, p_all, *_ffn_args(w, layer))


def _gmlp_tm(x2d, n_t, w, layer, j):
    rows, d = x2d.shape
    width = w["gm_w_out"].shape[1]
    out_shape = [jax.ShapeDtypeStruct((rows, d), _F32), jax.ShapeDtypeStruct((rows // n_t, n_t, width), _F32)]
    return pl.pallas_call(
        functools.partial(_gmlp_tm_kernel, layer, j, n_t),
        grid=(1,),
        in_specs=[_full_spec(x2d.shape), _whole(w["norm_mix"]), _layer_of(w["gm_w_in"], j), _whole(w["gm_ln_g"]),
                  _whole(w["gm_ln_b"]), _layer_of(w["gm_w_s_head"], j), _layer_of(w["gm_b_s_head"], j),
                  _layer_of(w["gm_w_out"], j)],
        out_specs=[_full_spec(s.shape) for s in out_shape],
        out_shape=out_shape,
        scratch_shapes=[pltpu.VMEM((rows, width), _F32)],
        compiler_params=_params(1),
        name="gmlp_tm",
    )(x2d, w["norm_mix"], w["gm_w_in"], w["gm_ln_g"], w["gm_ln_b"], w["gm_w_s_head"], w["gm_b_s_head"],
      w["gm_w_out"])


def _ffn_tm(x2d, p_all, past_all, w, layer, final):
    rows, d = x2d.shape
    _, bsz, n_t, pdim = p_all.shape
    n_keep, d_ff = past_all.shape[2], past_all.shape[3]
    w_up, w_down, w_gate, w_proj = w["ffn_w_up"], w["ffn_w_down"], w["ple_w_gate"], w["ple_w_proj"]
    kw = w["ffn_conv_w"].shape[1]
    n_layers = w["ffn_conv_b"].shape[0]
    cn = min(_FFN_TM_CHUNK, d_ff)
    n_chunks = d_ff // cn
    assert d_ff % cn == 0
    o_shape = (bsz, n_t, d) if final else (rows, d)
    out_shape = [jax.ShapeDtypeStruct(o_shape, _F32), jax.ShapeDtypeStruct(past_all.shape[1:], _F32),
                 jax.ShapeDtypeStruct((d, d_ff), _BF16), jax.ShapeDtypeStruct((d, d_ff), _BF16),
                 jax.ShapeDtypeStruct((d_ff, d), _BF16), jax.ShapeDtypeStruct(w_gate.shape[1:], _BF16),
                 jax.ShapeDtypeStruct(w_proj.shape[1:], _BF16)]
    in_specs = [
        _full_spec(x2d.shape), _layer_of(p_all, layer),
        pl.BlockSpec((None, bsz, n_keep, cn), lambda j: (layer, 0, 0, j)),
        _whole(w["norm_ffn"]),
        pl.BlockSpec((None, d, cn), lambda j: (layer, 0, j)),
        pl.BlockSpec((None, d, cn), lambda j: (layer, 0, n_chunks + j)),
        pl.BlockSpec((None, kw, cn), lambda j: (layer, 0, j)),
        pl.BlockSpec((n_layers, cn), lambda j: (0, j)),
        pl.BlockSpec((None, cn, d), lambda j: (layer, j, 0)),
        _whole(w["norm_ple"]), _layer_of(w_gate, layer), _layer_of(w_proj, layer), _whole(w["norm_final"]),
    ]
    out_specs = [
        _full_spec(o_shape), pl.BlockSpec((bsz, n_keep, cn), lambda j: (0, 0, j)),
        pl.BlockSpec((d, cn), lambda j: (0, j)), pl.BlockSpec((d, cn), lambda j: (0, j)),
        pl.BlockSpec((cn, d), lambda j: (j, 0)), _full_spec(w_gate.shape[1:]), _full_spec(w_proj.shape[1:]),
    ]
    outs = pl.pallas_call(
        functools.partial(_ffn_tm_kernel, layer, final),
        grid=(n_chunks,),
        in_specs=in_specs,
        out_specs=out_specs,
        out_shape=out_shape,
        scratch_shapes=[pltpu.VMEM((rows, d), _BF16), pltpu.VMEM((rows, d), _F32)],
        compiler_params=_params(1),
        name="ffn_tm",
    )(x2d, p_all, past_all, w["norm_ffn"], w_up, w_up, w["ffn_conv_w"], w["ffn_conv_b"], w_down,
      w["norm_ple"], w_gate, w_proj, w["norm_final"])
    return outs[0], outs[1], tuple(outs[2:])


def kernel(x_prompt, x_sample, p_prompt, p_sample, state_lru_h, state_lru_conv, state_ffn_conv, norm_mix, norm_ffn, norm_ple, norm_final, lru_w_in, lru_conv_w, lru_conv_b, lru_w_a, lru_b_a, lru_w_x, lru_b_x, lru_lambda, lru_w_out, gm_w_in, gm_ln_g, gm_ln_b, gm_w_s, gm_b_s, gm_w_out, ffn_w_up, ffn_conv_w, ffn_conv_b, ffn_w_down, ple_w_gate, ple_w_proj):
    depth = norm_mix.shape[0]
    bsz, seq, d_model = x_prompt.shape
    dec_b, dec_t, _ = x_sample.shape
    d_ff = ffn_w_down.shape[1]
    groups, chunk = gm_w_s.shape[1], gm_w_s.shape[2]
    gd = gm_w_out.shape[1] // groups
    tm = min(_GMLP_TILE, seq)
    assert seq % tm == 0 and tm % chunk == 0 and tm % _SUBLANES == 0 and d_ff % min(_FFN_CHUNK, d_ff) == 0
    assert _PAST_LEN % chunk == 0 and dec_t <= chunk and dec_b % _SUBLANES == 0
    assert seq % min(_FFN_TILE, seq) == 0 and depth % 2 == 0

    w = dict(
        norm_mix=norm_mix, norm_ffn=norm_ffn, norm_ple=norm_ple, norm_final=norm_final.reshape(1, -1),
        lru_w_in=lru_w_in.astype(_BF16), lru_conv_w=lru_conv_w, lru_conv_b=lru_conv_b,
        lru_w_a=(0.5 * lru_w_a).astype(_BF16), lru_b_a=0.5 * lru_b_a,
        lru_w_x=(0.5 * lru_w_x).astype(_BF16), lru_b_x=0.5 * lru_b_x,
        lru_lambda=lru_lambda, lru_w_out=lru_w_out.astype(_BF16),
        gm_w_in=gm_w_in.astype(_BF16), gm_ln_g=gm_ln_g, gm_ln_b=gm_ln_b, gm_w_s=gm_w_s,
        gm_w_out=gm_w_out.astype(_BF16),
        gm_b_s_rows=jnp.repeat(jnp.transpose(gm_b_s, (0, 2, 1)), gd, axis=2),
        gm_w_s_head=jnp.repeat(jnp.transpose(gm_w_s[:, :, :dec_t, :dec_t], (0, 2, 3, 1)), gd, axis=3),
        gm_b_s_head=jnp.repeat(jnp.transpose(gm_b_s[:, :, :dec_t], (0, 2, 1)), gd, axis=2),
        ffn_w_up=ffn_w_up, ffn_conv_w=ffn_conv_w, ffn_conv_b=ffn_conv_b, ffn_w_down=ffn_w_down,
        ple_w_gate=ple_w_gate, ple_w_proj=ple_w_proj, ffn_bf16={},
    )

    hp = x_prompt
    hs = x_sample
    lru_h_p, lru_conv_p, ffn_conv_p = [], [], []
    lru_h_s, lru_conv_s, ffn_conv_s, gm_v_s = [], [], [], []
    for i in range(depth):
        j = i // 2
        if i % 2 == 0:
            kw = lru_conv_w.shape[1]
            width = lru_w_out.shape[1]
            hp, hlast, tail = _lru(hp, jnp.zeros((bsz, width), _F32), jnp.zeros((bsz, kw - 1, width), _F32),
                                   w, i, j, True)
            lru_h_p.append(hlast)
            lru_conv_p.append(tail)
            if hs.ndim == 2:
                hs = _from_tm(hs, dec_t)
            hs, hlast, tail = _lru(hs, state_lru_h[j], state_lru_conv[j], w, i, j, False)
            lru_h_s.append(hlast)
            lru_conv_s.append(tail)
        else:
            hp = _gmlp_seq(hp, w, i, j)
            hs, v = _gmlp_tm(hs, dec_t, w, i, j)
            gm_v_s.append(v)
        kw = ffn_conv_w.shape[1]
        final = i == depth - 1
        hs, tail, w["ffn_bf16"][i] = _ffn_tm(hs, p_sample, state_ffn_conv, w, i, final)
        ffn_conv_s.append(tail)
        hp, tail = _ffn_seq(hp, p_prompt, w, i, final, kw - 1)
        ffn_conv_p.append(tail)

    def stack(xs):
        return xs[0][None] if len(xs) == 1 else jnp.stack(xs)

    return (hp, hs, stack(lru_h_p), stack(lru_conv_p), stack(ffn_conv_p),
            stack(lru_h_s), stack(lru_conv_s), stack(ffn_conv_s), stack(gm_v_s))
```

```python
import functools
import math

import jax
import jax.numpy as jnp
from jax import lax
from jax.experimental import pallas as pl
from jax.experimental.pallas import tpu as pltpu

_EPS = 1e-6
_LRU_C = 8.0
_PAST_LEN = 16384
_SUBLANES = 8
_LRU_TILE = 1024
_GMLP_TILE = 1024
_FFN_TILE = 1024
_FFN_CHUNK = 1024
_PLE_ROWS = 256
_LRU_BLOCKS = 4
_FFN_TM_CHUNK = 512
_VMEM_LIMIT = 56 * 1024 * 1024

_F32 = jnp.float32
_BF16 = jnp.bfloat16


def _mm(a, b):
    return jnp.dot(a, b, preferred_element_type=_F32)


def _vec(ref, i):
    return ref[i:i + 1, :]


def _to_tm(a):
    b, t, c = a.shape
    return jnp.swapaxes(a, 0, 1).reshape(t * b, c)


def _from_tm(a2d, n_t):
    rows, c = a2d.shape
    return jnp.swapaxes(a2d.reshape(n_t, rows // n_t, c), 0, 1)


def _rmsnorm(x, g):
    ms = jnp.mean(x * x, axis=-1, keepdims=True)
    return (x * lax.rsqrt(ms + _EPS)) * g


def _layernorm(x, g, b):
    mu = jnp.mean(x, axis=-1, keepdims=True)
    xc = x - mu
    y = xc * lax.rsqrt(jnp.mean(xc * xc, axis=-1, keepdims=True) + _EPS)
    return y * g + b


def _gelu(x):
    c = math.sqrt(2.0 / math.pi)
    return x * (0.5 * (1.0 + jnp.tanh(c * (x + 0.044715 * (x * x * x)))))


def _softplus(x):
    return jnp.maximum(x, 0.0) + jnp.log1p(jnp.exp(-jnp.abs(x)))


def _conv_rows(x, prev8, w, b):
    kw = w.shape[0]
    x0 = x[0:_SUBLANES]
    row = lax.broadcasted_iota(jnp.int32, x0.shape, 0)
    y = None
    y0 = None
    for k in range(kw):
        d = kw - 1 - k
        wk = w[k:k + 1]
        if d == 0:
            t, t0 = x * wk, x0 * wk
        else:
            t = pltpu.roll(x, d, 0) * wk
            t0 = jnp.where(row < d, pltpu.roll(prev8, d, 0), pltpu.roll(x0, d, 0)) * wk
        y = t if y is None else y + t
        y0 = t0 if y0 is None else y0 + t0
    return jnp.concatenate([y0, y[_SUBLANES:]], axis=0) + b


def _conv_tm(x, past, w, b):
    kw = w.shape[0]
    rows = x.shape[0]
    rb = past.shape[0] // (kw - 1)
    ext = jnp.concatenate([past, x], axis=0)
    y = ext[0:rows] * w[0:1]
    for k in range(1, kw):
        y = y + ext[k * rb:k * rb + rows] * w[k:k + 1]
    return y + b, ext[rows:]


def _sigmoid(z):
    return 0.5 * jnp.tanh(0.5 * z) + 0.5


def _lru_coeffs(xh, wa_half, ba_half, wx_half, bx_half, half_decay):
    xhb = xh.astype(_BF16)
    t_r = jnp.tanh(_mm(xhb, wa_half) + ba_half)
    t_i = jnp.tanh(_mm(xhb, wx_half) + bx_half)
    m = t_r * half_decay + half_decay
    a = jnp.exp(-m)
    u = jnp.tanh(m) * (1.0 + a * a)
    mult = jnp.where(u > 0.0, u * lax.rsqrt(u), 0.0)
    hx = 0.5 * xh
    return a, mult * (t_i * hx + hx)


def _scan_slabs(a_s, b_s, h_s, carry):
    rb = carry.shape[0]
    n_t = a_s.shape[0] // rb

    def body(t, h):
        rs = pl.ds(pl.multiple_of(t * rb, rb), rb)
        h = a_s[rs, :] * h + b_s[rs, :]
        h_s[rs, :] = h
        return h

    return lax.fori_loop(0, n_t, body, carry, unroll=min(n_t, _SUBLANES))


def _ffn_ple(layer, final, x_ref, p_ref, conv_fn, nf_ref, wg_ref, wu_ref, cw_ref, cb_ref, w_down_ref,
             np_ref, w_gate_ref, w_proj_ref, nfin_ref, o_ref, acc_s):
    rows = x_ref.shape[0]
    d_ff = w_down_ref.shape[0]
    cn = min(_FFN_CHUNK, d_ff)
    rbk = min(_PLE_ROWS, rows)
    blocks = [slice(r * rbk, (r + 1) * rbk) for r in range(rows // rbk)]
    xn_blocks = [_rmsnorm(x_ref[rs, :], _vec(nf_ref, layer)).astype(_BF16) for rs in blocks]
    xn = jnp.concatenate(xn_blocks, axis=0)
    for j in range(d_ff // cn):
        cs = slice(j * cn, (j + 1) * cn)
        if j == 0:
            g = jnp.concatenate([_mm(xb, wg_ref[:, cs]) for xb in xn_blocks], axis=0)
            u = jnp.concatenate([_mm(xb, wu_ref[:, cs]) for xb in xn_blocks], axis=0)
        else:
            g = _mm(xn, wg_ref[:, cs])
            u = _mm(xn, wu_ref[:, cs])
        gc = conv_fn(cs, g, cw_ref[:, cs], cb_ref[layer:layer + 1, cs])
        d = _mm((_gelu(gc) * u).astype(_BF16), w_down_ref[cs, :])
        if j == 0:
            acc_s[...] = d
        else:
            acc_s[...] += d
    for rs in blocks:
        h = x_ref[rs, :] + acc_s[rs, :]
        gate = _sigmoid(_mm(_rmsnorm(h, _vec(np_ref, layer)).astype(_BF16), w_gate_ref[...]))
        h = h + gate * _mm(p_ref[rs, :].astype(_BF16), w_proj_ref[...])
        o_ref[rs, :] = _rmsnorm(h, nfin_ref[...]) if final else h


def _lru_kernel(layer, j, cast, x_ref, h0_ref, past_ref, g_ref, w_in_ref, cw_ref, cb_ref, wa_ref, ba_ref,
                wx_ref, bx_ref, lam_ref, w_out_ref, o_ref, hlast_ref, tail_ref, *rest):
    if cast:
        w_in_bf, wa_bf, wx_bf, w_out_bf, tail_s, xb_s, gate_s, a_s, b_s, h_s = rest
        w_in_bf[...] = w_in_ref[...].astype(_BF16)
        wa_bf[...] = (0.5 * wa_ref[...]).astype(_BF16)
        wx_bf[...] = (0.5 * wx_ref[...]).astype(_BF16)
        w_out_bf[...] = w_out_ref[...].astype(_BF16)
        w_in_ref, wa_ref, wx_ref, w_out_ref = w_in_bf, wa_bf, wx_bf, w_out_bf
    else:
        tail_s, xb_s, gate_s, a_s, b_s, h_s = rest
    @pl.when(pl.program_id(0) == 0)
    def _():
        hlast_ref[...] = h0_ref[...]
        tail_s[...] = _to_tm(past_ref[...])

    heads, dh, _ = wa_ref.shape
    width = heads * dh
    bsz, tt, _ = x_ref.shape
    n_blk = _LRU_BLOCKS if tt % (_LRU_BLOCKS * _SUBLANES) == 0 else 1
    tb = tt // n_blk
    for r in range(n_blk):
        rs = slice(r * tb * bsz, (r + 1) * tb * bsz)
        x = _to_tm(x_ref[:, r * tb:(r + 1) * tb, :])
        xn = _rmsnorm(x, _vec(g_ref, layer)).astype(_BF16)
        xb_s[rs, :] = _mm(xn, w_in_ref[:, width:])
        gate_s[rs, :] = _mm(xn, w_in_ref[:, :width])
    half_decay = (0.5 * _LRU_C) * _softplus(-_vec(lam_ref, j))
    xc, tail = _conv_tm(xb_s[...], tail_s[...], cw_ref[...], _vec(cb_ref, j))
    tail_s[...] = tail
    tail_ref[...] = _from_tm(tail, tail_ref.shape[1])
    for hh in range(heads):
        sl = slice(hh * dh, (hh + 1) * dh)
        a, b = _lru_coeffs(xc[:, sl], wa_ref[hh], ba_ref[j:j + 1, sl], wx_ref[hh], bx_ref[j:j + 1, sl],
                           half_decay[:, sl])
        a_s[:, sl] = a
        b_s[:, sl] = b
    hlast_ref[...] = _scan_slabs(a_s, b_s, h_s, hlast_ref[...])
    for r in range(n_blk):
        rs = slice(r * tb * bsz, (r + 1) * tb * bsz)
        y = _mm((_gelu(gate_s[rs, :]) * h_s[rs, :]).astype(_BF16), w_out_ref[...])
        if len(o_ref.shape) == 3:
            ts = slice(r * tb, (r + 1) * tb)
            o_ref[:, ts, :] = x_ref[:, ts, :] + _from_tm(y, tb)
        else:
            o_ref[rs, :] = _to_tm(x_ref[:, r * tb:(r + 1) * tb, :]) + y


def _gmlp_seq_kernel(layer, j, x_ref, g_ref, w_in_ref, lng_ref, lnb_ref, ws_ref, bs_ref, w_out_ref,
                     o_ref, s_s):
    x = x_ref[...]
    rows = x.shape[0]
    groups, chunk, _ = ws_ref.shape
    width = s_s.shape[1]
    gd = width // groups
    xn = _rmsnorm(x, _vec(g_ref, layer)).astype(_BF16)
    n_piece = 2
    pw = width // n_piece
    zv = [_mm(xn, w_in_ref[:, width + k * pw:width + (k + 1) * pw]) for k in range(n_piece)]
    zu = [_mm(xn, w_in_ref[:, k * pw:(k + 1) * pw]) for k in range(n_piece)]
    v = jnp.concatenate([_gelu(z) for z in zv], axis=1)
    v = _layernorm(v, _vec(lng_ref, j), _vec(lnb_ref, j)).astype(_BF16)
    tri = (lax.broadcasted_iota(jnp.int32, (chunk, chunk), 0)
           >= lax.broadcasted_iota(jnp.int32, (chunk, chunk), 1))
    n_c = rows // chunk
    for g in range(groups):
        wg = jnp.where(tri, ws_ref[g], 0.0).astype(_BF16)
        cs = slice(g * gd, (g + 1) * gd)
        vg = jnp.concatenate([v[c * chunk:(c + 1) * chunk, cs] for c in range(n_c)], axis=1)
        sg = _mm(wg, vg)
        for c in range(n_c):
            s_s[c * chunk:(c + 1) * chunk, cs] = sg[:, c * gd:(c + 1) * gd] + bs_ref[:, cs]
    u = jnp.concatenate([_gelu(z) for z in zu], axis=1)
    o_ref[...] = x + _mm((u * s_s[...]).astype(_BF16), w_out_ref[...])


def _ffn_seq_kernel(layer, final, x_ref, p_ref, nf_ref, wg_ref, wu_ref, cw_ref, cb_ref, w_down_ref,
                    np_ref, w_gate_ref, w_proj_ref, nfin_ref, o_ref, tail_ref, prev_s, acc_s):
    @pl.when(pl.program_id(1) == 0)
    def _():
        prev_s[...] = jnp.zeros_like(prev_s)

    rows = x_ref.shape[0]
    keep = tail_ref.shape[0]

    def conv(cs, g, w, b):
        gc = _conv_rows(g, prev_s[:, cs], w, b)
        prev_s[:, cs] = g[rows - _SUBLANES:rows, :]
        tail_ref[:, cs] = g[rows - keep:rows, :]
        return gc

    _ffn_ple(layer, final, x_ref, p_ref, conv, nf_ref, wg_ref, wu_ref, cw_ref, cb_ref, w_down_ref,
             np_ref, w_gate_ref, w_proj_ref, nfin_ref, o_ref, acc_s)


def _gmlp_tm_kernel(layer, j, n_t, x_ref, g_ref, w_in_ref, lng_ref, lnb_ref, wsc_ref, bsc_ref, w_out_ref,
                    o_ref, v_ref, w_in_bf, w_out_bf, s_s):
    w_in_bf[...] = w_in_ref[...].astype(_BF16)
    w_out_bf[...] = w_out_ref[...].astype(_BF16)
    w_in_ref, w_out_ref = w_in_bf, w_out_bf
    x = x_ref[...]
    width = s_s.shape[1]
    bs = x.shape[0] // n_t
    z = _gelu(_mm(_rmsnorm(x, _vec(g_ref, layer)).astype(_BF16), w_in_ref[...]))
    u = z[:, :width]
    v = _layernorm(z[:, width:], _vec(lng_ref, j), _vec(lnb_ref, j))
    v_ref[...] = _from_tm(v, n_t)
    for t in range(n_t):
        s = bsc_ref[t:t + 1, :]
        for q in range(t + 1):
            s = s + wsc_ref[t, q:q + 1, :] * v[q * bs:(q + 1) * bs, :]
        s_s[t * bs:(t + 1) * bs, :] = s
    o_ref[...] = x + _mm((u * s_s[...]).astype(_BF16), w_out_ref[...])


def _ffn_tm_kernel(layer, final, x_ref, p_ref, past_ref, nf_ref, wg_ref, wu_ref, cw_ref, cb_ref, wd_ref,
                   np_ref, w_gate_ref, w_proj_ref, nfin_ref, o_ref, tail_ref, wg_bf_ref, wu_bf_ref,
                   wd_bf_ref, gate_bf_ref, proj_bf_ref, xn_s, acc_s):
    j = pl.program_id(0)
    n_keep = past_ref.shape[1]

    @pl.when(j == 0)
    def _():
        xn_s[...] = _rmsnorm(x_ref[...], _vec(nf_ref, layer)).astype(_BF16)
        acc_s[...] = jnp.zeros_like(acc_s)

    wg = wg_ref[...].astype(_BF16)
    wu = wu_ref[...].astype(_BF16)
    wd = wd_ref[...].astype(_BF16)
    wg_bf_ref[...] = wg
    wu_bf_ref[...] = wu
    wd_bf_ref[...] = wd
    xn = xn_s[...]
    g = _mm(xn, wg)
    u = _mm(xn, wu)
    gc, tail = _conv_tm(g, _to_tm(past_ref[...]), cw_ref[...], cb_ref[layer:layer + 1, :])
    tail_ref[...] = _from_tm(tail, n_keep)
    acc_s[...] += _mm((_gelu(gc) * u).astype(_BF16), wd)

    @pl.when(j == pl.num_programs(0) - 1)
    def _():
        w_gate = w_gate_ref[...].astype(_BF16)
        w_proj = w_proj_ref[...].astype(_BF16)
        gate_bf_ref[...] = w_gate
        proj_bf_ref[...] = w_proj
        h = x_ref[...] + acc_s[...]
        gate = _sigmoid(_mm(_rmsnorm(h, _vec(np_ref, layer)).astype(_BF16), w_gate))
        h = h + gate * _mm(_to_tm(p_ref[...]).astype(_BF16), w_proj)
        if final:
            o_ref[...] = _from_tm(_rmsnorm(h, nfin_ref[...]), o_ref.shape[1])
        else:
            o_ref[...] = h


def _whole(a):
    zeros = (0,) * a.ndim
    return pl.BlockSpec(a.shape, lambda *_: zeros, pipeline_mode=pl.Buffered(1))


def _layer_of(a, i):
    zeros = (0,) * (a.ndim - 1)
    return pl.BlockSpec((None,) + a.shape[1:], lambda *_: (i,) + zeros, pipeline_mode=pl.Buffered(1))


def _params(n_axes):
    return pltpu.CompilerParams(dimension_semantics=("arbitrary",) * n_axes, vmem_limit_bytes=_VMEM_LIMIT)


def _row_spec(tm, d):
    return pl.BlockSpec((None, tm, d), lambda b, t: (b, t, 0))


def _state_spec(shape):
    zeros = (0,) * len(shape)
    return pl.BlockSpec((None,) + shape, lambda b, t: (b,) + zeros)


def _full_spec(shape):
    zeros = (0,) * len(shape)
    return pl.BlockSpec(shape, lambda *_: zeros)


def _lru_specs(w, j, prompt):
    if prompt:
        mats = [_whole(a) for a in w["lru_bf16"][j]]
    else:
        mats = [_layer_of(w[k], j) for k in ("lru_w_in", "lru_w_a", "lru_w_x", "lru_w_out")]
    return [_whole(w["norm_mix"]), mats[0], _layer_of(w["lru_conv_w"], j), _whole(w["lru_conv_b"]),
            mats[1], _whole(w["lru_b_a"]), mats[2], _whole(w["lru_b_x"]), _whole(w["lru_lambda"]), mats[3]]


def _lru_args(w, j, prompt):
    mats = w["lru_bf16"][j] if prompt else [w[k] for k in ("lru_w_in", "lru_w_a", "lru_w_x", "lru_w_out")]
    return [w["norm_mix"], mats[0], w["lru_conv_w"], w["lru_conv_b"], mats[1], w["lru_b_a"], mats[2],
            w["lru_b_x"], w["lru_lambda"], mats[3]]


def _ffn_specs(w, i):
    wg, wu, wd, w_gate, w_proj = w["ffn_bf16"][i]
    return [_whole(w["norm_ffn"]), _whole(wg), _whole(wu), _layer_of(w["ffn_conv_w"], i),
            _whole(w["ffn_conv_b"]), _whole(wd), _whole(w["norm_ple"]), _whole(w_gate), _whole(w_proj),
            _whole(w["norm_final"])]


def _ffn_args(w, i):
    wg, wu, wd, w_gate, w_proj = w["ffn_bf16"][i]
    return [w["norm_ffn"], wg, wu, w["ffn_conv_w"], w["ffn_conv_b"], wd, w["norm_ple"], w_gate, w_proj,
            w["norm_final"]]


def _lru(x, h0, past, w, layer, j, prompt):
    bsz, t, d = x.shape
    width = w["lru_w_out"].shape[1]
    if prompt:
        tt = min(_LRU_TILE // bsz, t)
        assert tt % _SUBLANES == 0 and t % tt == 0 and bsz % _SUBLANES == 0
        o_shape, o_spec = x.shape, pl.BlockSpec((bsz, tt, d), lambda t: (0, t, 0))
        copies = []
    else:
        tt = t
        o_shape, o_spec = (t * bsz, d), _full_spec((t * bsz, d))
        copies = [jax.ShapeDtypeStruct(w[k].shape[1:], _BF16) for k in ("lru_w_in", "lru_w_a", "lru_w_x", "lru_w_out")]
    out_shape = [jax.ShapeDtypeStruct(o_shape, _F32), jax.ShapeDtypeStruct(h0.shape, _F32),
                 jax.ShapeDtypeStruct(past.shape, _F32)] + copies
    outs = pl.pallas_call(
        functools.partial(_lru_kernel, layer, j, not prompt),
        grid=(t // tt,),
        in_specs=[pl.BlockSpec((bsz, tt, d), lambda t: (0, t, 0)), _whole(h0), _whole(past)]
        + _lru_specs(w, j, prompt),
        out_specs=[o_spec, _full_spec(h0.shape), _full_spec(past.shape)] + [_full_spec(c.shape) for c in copies],
        out_shape=out_shape,
        scratch_shapes=[pltpu.VMEM((past.shape[1] * bsz, width), _F32)]
        + [pltpu.VMEM((bsz * tt, width), _F32)] * 5,
        compiler_params=_params(1),
        name="lru_seq" if prompt else "lru_tm",
    )(x, h0, past, *_lru_args(w, j, prompt))
    return outs[0], outs[1], outs[2], tuple(outs[3:])


def _gmlp_seq(x, w, layer, j):
    bsz, t, d = x.shape
    w_in, w_out = w["gm_bf16"][j]
    width = w_out.shape[0]
    tm = min(_GMLP_TILE, t)
    return pl.pallas_call(
        functools.partial(_gmlp_seq_kernel, layer, j),
        grid=(bsz, t // tm),
        in_specs=[_row_spec(tm, d), _whole(w["norm_mix"]), _whole(w_in), _whole(w["gm_ln_g"]),
                  _whole(w["gm_ln_b"]), _layer_of(w["gm_w_s"], j), _layer_of(w["gm_b_s_rows"], j),
                  _whole(w_out)],
        out_specs=_row_spec(tm, d),
        out_shape=jax.ShapeDtypeStruct((bsz, t, d), _F32),
        scratch_shapes=[pltpu.VMEM((tm, width), _F32)],
        compiler_params=_params(2),
        name="gmlp_seq",
    )(x, w["norm_mix"], w_in, w["gm_ln_g"], w["gm_ln_b"], w["gm_w_s"], w["gm_b_s_rows"], w_out)


def _ffn_seq(x, p_all, w, layer, final, n_keep):
    bsz, t, d = x.shape
    pdim = p_all.shape[-1]
    d_ff = w["ffn_conv_w"].shape[2]
    tm = min(_FFN_TILE, t)
    p_spec = pl.BlockSpec((None, None, tm, pdim), lambda b, t: (layer, b, t, 0))
    return pl.pallas_call(
        functools.partial(_ffn_seq_kernel, layer, final),
        grid=(bsz, t // tm),
        in_specs=[_row_spec(tm, d), p_spec] + _ffn_specs(w, layer),
        out_specs=[_row_spec(tm, d), _state_spec((n_keep, d_ff))],
        out_shape=[jax.ShapeDtypeStruct((bsz, t, d), _F32),
                   jax.ShapeDtypeStruct((bsz, n_keep, d_ff), _F32)],
        scratch_shapes=[pltpu.VMEM((_SUBLANES, d_ff), _F32), pltpu.VMEM((tm, d), _F32)],
        compiler_params=_params(2),
        name="ffn_seq",
    )(x, p_all, *_ffn_args(w, layer))


def _gmlp_tm(x2d, n_t, w, layer, j):
    rows, d = x2d.shape
    w_in, w_out = w["gm_w_in"], w["gm_w_out"]
    width = w_out.shape[1]
    out_shape = [jax.ShapeDtypeStruct((rows, d), _F32), jax.ShapeDtypeStruct((rows // n_t, n_t, width), _F32),
                 jax.ShapeDtypeStruct(w_in.shape[1:], _BF16), jax.ShapeDtypeStruct(w_out.shape[1:], _BF16)]
    outs = pl.pallas_call(
        functools.partial(_gmlp_tm_kernel, layer, j, n_t),
        grid=(1,),
        in_specs=[_full_spec(x2d.shape), _whole(w["norm_mix"]), _layer_of(w_in, j), _whole(w["gm_ln_g"]),
                  _whole(w["gm_ln_b"]), _layer_of(w["gm_w_s_head"], j), _layer_of(w["gm_b_s_head"], j),
                  _layer_of(w_out, j)],
        out_specs=[_full_spec(s.shape) for s in out_shape],
        out_shape=out_shape,
        scratch_shapes=[pltpu.VMEM((rows, width), _F32)],
        compiler_params=_params(1),
        name="gmlp_tm",
    )(x2d, w["norm_mix"], w_in, w["gm_ln_g"], w["gm_ln_b"], w["gm_w_s_head"], w["gm_b_s_head"], w_out)
    return outs[0], outs[1], tuple(outs[2:])


def _ffn_tm(x2d, p_all, past_all, w, layer, final):
    rows, d = x2d.shape
    _, bsz, n_t, pdim = p_all.shape
    n_keep, d_ff = past_all.shape[2], past_all.shape[3]
    w_up, w_down, w_gate, w_proj = w["ffn_w_up"], w["ffn_w_down"], w["ple_w_gate"], w["ple_w_proj"]
    kw = w["ffn_conv_w"].shape[1]
    n_layers = w["ffn_conv_b"].shape[0]
    cn = min(_FFN_TM_CHUNK, d_ff)
    n_chunks = d_ff // cn
    assert d_ff % cn == 0
    o_shape = (bsz, n_t, d) if final else (rows, d)
    out_shape = [jax.ShapeDtypeStruct(o_shape, _F32), jax.ShapeDtypeStruct(past_all.shape[1:], _F32),
                 jax.ShapeDtypeStruct((d, d_ff), _BF16), jax.ShapeDtypeStruct((d, d_ff), _BF16),
                 jax.ShapeDtypeStruct((d_ff, d), _BF16), jax.ShapeDtypeStruct(w_gate.shape[1:], _BF16),
                 jax.ShapeDtypeStruct(w_proj.shape[1:], _BF16)]
    in_specs = [
        _full_spec(x2d.shape), _layer_of(p_all, layer),
        pl.BlockSpec((None, bsz, n_keep, cn), lambda j: (layer, 0, 0, j)),
        _whole(w["norm_ffn"]),
        pl.BlockSpec((None, d, cn), lambda j: (layer, 0, j)),
        pl.BlockSpec((None, d, cn), lambda j: (layer, 0, n_chunks + j)),
        pl.BlockSpec((None, kw, cn), lambda j: (layer, 0, j)),
        pl.BlockSpec((n_layers, cn), lambda j: (0, j)),
        pl.BlockSpec((None, cn, d), lambda j: (layer, j, 0)),
        _whole(w["norm_ple"]), _layer_of(w_gate, layer), _layer_of(w_proj, layer), _whole(w["norm_final"]),
    ]
    out_specs = [
        _full_spec(o_shape), pl.BlockSpec((bsz, n_keep, cn), lambda j: (0, 0, j)),
        pl.BlockSpec((d, cn), lambda j: (0, j)), pl.BlockSpec((d, cn), lambda j: (0, j)),
        pl.BlockSpec((cn, d), lambda j: (j, 0)), _full_spec(w_gate.shape[1:]), _full_spec(w_proj.shape[1:]),
    ]
    outs = pl.pallas_call(
        functools.partial(_ffn_tm_kernel, layer, final),
        grid=(n_chunks,),
        in_specs=in_specs,
        out_specs=out_specs,
        out_shape=out_shape,
        scratch_shapes=[pltpu.VMEM((rows, d), _BF16), pltpu.VMEM((rows, d), _F32)],
        compiler_params=_params(1),
        name="ffn_tm",
    )(x2d, p_all, past_all, w["norm_ffn"], w_up, w_up, w["ffn_conv_w"], w["ffn_conv_b"], w_down,
      w["norm_ple"], w_gate, w_proj, w["norm_final"])
    return outs[0], outs[1], tuple(outs[2:])


def kernel(x_prompt, x_sample, p_prompt, p_sample, state_lru_h, state_lru_conv, state_ffn_conv, norm_mix, norm_ffn, norm_ple, norm_final, lru_w_in, lru_conv_w, lru_conv_b, lru_w_a, lru_b_a, lru_w_x, lru_b_x, lru_lambda, lru_w_out, gm_w_in, gm_ln_g, gm_ln_b, gm_w_s, gm_b_s, gm_w_out, ffn_w_up, ffn_conv_w, ffn_conv_b, ffn_w_down, ple_w_gate, ple_w_proj):
    depth = norm_mix.shape[0]
    bsz, seq, d_model = x_prompt.shape
    dec_b, dec_t, _ = x_sample.shape
    d_ff = ffn_w_down.shape[1]
    groups, chunk = gm_w_s.shape[1], gm_w_s.shape[2]
    gd = gm_w_out.shape[1] // groups
    tm = min(_GMLP_TILE, seq)
    assert seq % tm == 0 and tm % chunk == 0 and tm % _SUBLANES == 0 and d_ff % min(_FFN_CHUNK, d_ff) == 0
    assert _PAST_LEN % chunk == 0 and dec_t <= chunk and dec_b % _SUBLANES == 0
    assert seq % min(_FFN_TILE, seq) == 0 and depth % 2 == 0

    w = dict(
        norm_mix=norm_mix, norm_ffn=norm_ffn, norm_ple=norm_ple, norm_final=norm_final.reshape(1, -1),
        lru_w_in=lru_w_in, lru_conv_w=lru_conv_w, lru_conv_b=lru_conv_b, lru_w_a=lru_w_a, lru_w_x=lru_w_x,
        lru_b_a=0.5 * lru_b_a, lru_b_x=0.5 * lru_b_x, lru_lambda=lru_lambda, lru_w_out=lru_w_out,
        gm_w_in=gm_w_in, gm_ln_g=gm_ln_g, gm_ln_b=gm_ln_b, gm_w_s=gm_w_s, gm_w_out=gm_w_out,
        gm_b_s_rows=jnp.repeat(jnp.transpose(gm_b_s, (0, 2, 1)), gd, axis=2),
        gm_w_s_head=jnp.repeat(jnp.transpose(gm_w_s[:, :, :dec_t, :dec_t], (0, 2, 3, 1)), gd, axis=3),
        gm_b_s_head=jnp.repeat(jnp.transpose(gm_b_s[:, :, :dec_t], (0, 2, 1)), gd, axis=2),
        ffn_w_up=ffn_w_up, ffn_conv_w=ffn_conv_w, ffn_conv_b=ffn_conv_b, ffn_w_down=ffn_w_down,
        ple_w_gate=ple_w_gate, ple_w_proj=ple_w_proj, lru_bf16={}, gm_bf16={}, ffn_bf16={},
    )

    hp = x_prompt
    hs = x_sample
    lru_h_p, lru_conv_p, ffn_conv_p = [], [], []
    lru_h_s, lru_conv_s, ffn_conv_s, gm_v_s = [], [], [], []
    for i in range(depth):
        j = i // 2
        if i % 2 == 0:
            kw = lru_conv_w.shape[1]
            width = lru_w_out.shape[1]
            if hs.ndim == 2:
                hs = _from_tm(hs, dec_t)
            hs, hlast, tail, w["lru_bf16"][j] = _lru(hs, state_lru_h[j], state_lru_conv[j], w, i, j, False)
            lru_h_s.append(hlast)
            lru_conv_s.append(tail)
            hp, hlast, tail, _ = _lru(hp, jnp.zeros((bsz, width), _F32), jnp.zeros((bsz, kw - 1, width), _F32),
                                      w, i, j, True)
            lru_h_p.append(hlast)
            lru_conv_p.append(tail)
        else:
            hs, v, w["gm_bf16"][j] = _gmlp_tm(hs, dec_t, w, i, j)
            gm_v_s.append(v)
            hp = _gmlp_seq(hp, w, i, j)
        kw = ffn_conv_w.shape[1]
        final = i == depth - 1
        hs, tail, w["ffn_bf16"][i] = _ffn_tm(hs, p_sample, state_ffn_conv, w, i, final)
        ffn_conv_s.append(tail)
        hp, tail = _ffn_seq(hp, p_prompt, w, i, final, kw - 1)
        ffn_conv_p.append(tail)

    def stack(xs):
        return xs[0][None] if len(xs) == 1 else jnp.stack(xs)

    return (hp, hs, stack(lru_h_p), stack(lru_conv_p), stack(ffn_conv_p),
            stack(lru_h_s), stack(lru_conv_s), stack(ffn_conv_s), stack(gm_v_s))
```

```python
import functools
import math

import jax
import jax.numpy as jnp
from jax import lax
from jax.experimental import pallas as pl
from jax.experimental.pallas import tpu as pltpu

_EPS = 1e-6
_LRU_C = 8.0
_PAST_LEN = 16384
_SUBLANES = 8
_LRU_TILE = 1024
_GMLP_TILE = 1024
_FFN_TILE = 1024
_FFN_CHUNK = 1024
_GMLP_BLOCKS = 4
_PLE_ROWS = 256
_LRU_BLOCKS = 4
_FFN_TM_CHUNK = 512
_VMEM_LIMIT = 56 * 1024 * 1024

_F32 = jnp.float32
_BF16 = jnp.bfloat16


def _mm(a, b):
    return jnp.dot(a, b, preferred_element_type=_F32)


def _vec(ref, i):
    return ref[i:i + 1, :]


def _to_tm(a):
    b, t, c = a.shape
    return jnp.swapaxes(a, 0, 1).reshape(t * b, c)


def _from_tm(a2d, n_t):
    rows, c = a2d.shape
    return jnp.swapaxes(a2d.reshape(n_t, rows // n_t, c), 0, 1)


def _rmsnorm(x, g):
    ms = jnp.mean(x * x, axis=-1, keepdims=True)
    return (x * lax.rsqrt(ms + _EPS)) * g


def _layernorm(x, g, b):
    mu = jnp.mean(x, axis=-1, keepdims=True)
    xc = x - mu
    y = xc * lax.rsqrt(jnp.mean(xc * xc, axis=-1, keepdims=True) + _EPS)
    return y * g + b


def _gelu(x):
    c = math.sqrt(2.0 / math.pi)
    return x * (0.5 * (1.0 + jnp.tanh(c * (x + 0.044715 * (x * x * x)))))


def _softplus(x):
    return jnp.maximum(x, 0.0) + jnp.log1p(jnp.exp(-jnp.abs(x)))


def _conv_rows(x, prev8, w, b):
    kw = w.shape[0]
    x0 = x[0:_SUBLANES]
    row = lax.broadcasted_iota(jnp.int32, x0.shape, 0)
    y = None
    y0 = None
    for k in range(kw):
        d = kw - 1 - k
        wk = w[k:k + 1]
        if d == 0:
            t, t0 = x * wk, x0 * wk
        else:
            t = pltpu.roll(x, d, 0) * wk
            t0 = jnp.where(row < d, pltpu.roll(prev8, d, 0), pltpu.roll(x0, d, 0)) * wk
        y = t if y is None else y + t
        y0 = t0 if y0 is None else y0 + t0
    return jnp.concatenate([y0, y[_SUBLANES:]], axis=0) + b


def _conv_tm(x, past, w, b):
    kw = w.shape[0]
    rows = x.shape[0]
    rb = past.shape[0] // (kw - 1)
    ext = jnp.concatenate([past, x], axis=0)
    y = ext[0:rows] * w[0:1]
    for k in range(1, kw):
        y = y + ext[k * rb:k * rb + rows] * w[k:k + 1]
    return y + b, ext[rows:]


def _sigmoid(z):
    return 0.5 * jnp.tanh(0.5 * z) + 0.5


def _lru_coeffs(xh, wa_half, ba_half, wx_half, bx_half, half_decay):
    xhb = xh.astype(_BF16)
    t_r = jnp.tanh(_mm(xhb, wa_half) + ba_half)
    t_i = jnp.tanh(_mm(xhb, wx_half) + bx_half)
    m = t_r * half_decay + half_decay
    a = jnp.exp(-m)
    u = jnp.tanh(m) * (1.0 + a * a)
    mult = jnp.where(u > 0.0, u * lax.rsqrt(u), 0.0)
    hx = 0.5 * xh
    return a, mult * (t_i * hx + hx)


def _scan_slabs(a_s, b_s, h_s, carry):
    rb = carry.shape[0]
    n_t = a_s.shape[0] // rb

    def body(t, h):
        rs = pl.ds(pl.multiple_of(t * rb, rb), rb)
        h = a_s[rs, :] * h + b_s[rs, :]
        h_s[rs, :] = h
        return h

    return lax.fori_loop(0, n_t, body, carry, unroll=min(n_t, _SUBLANES))


def _ffn_ple(layer, final, x_ref, p_ref, conv_fn, nf_ref, wg_ref, wu_ref, cw_ref, cb_ref, w_down_ref,
             np_ref, w_gate_ref, w_proj_ref, nfin_ref, o_ref, acc_s):
    rows = x_ref.shape[0]
    d_ff = w_down_ref.shape[0]
    cn = min(_FFN_CHUNK, d_ff)
    rbk = min(_PLE_ROWS, rows)
    blocks = [slice(r * rbk, (r + 1) * rbk) for r in range(rows // rbk)]
    xn_blocks = [_rmsnorm(x_ref[rs, :], _vec(nf_ref, layer)).astype(_BF16) for rs in blocks]
    xn = jnp.concatenate(xn_blocks, axis=0)
    for j in range(d_ff // cn):
        cs = slice(j * cn, (j + 1) * cn)
        if j == 0:
            g = jnp.concatenate([_mm(xb, wg_ref[:, cs]) for xb in xn_blocks], axis=0)
            u = jnp.concatenate([_mm(xb, wu_ref[:, cs]) for xb in xn_blocks], axis=0)
        else:
            g = _mm(xn, wg_ref[:, cs])
            u = _mm(xn, wu_ref[:, cs])
        gc = conv_fn(cs, g, cw_ref[:, cs], cb_ref[layer:layer + 1, cs])
        d = _mm((_gelu(gc) * u).astype(_BF16), w_down_ref[cs, :])
        if j == 0:
            acc_s[...] = d
        else:
            acc_s[...] += d
    for rs in blocks:
        h = x_ref[rs, :] + acc_s[rs, :]
        gate = _sigmoid(_mm(_rmsnorm(h, _vec(np_ref, layer)).astype(_BF16), w_gate_ref[...]))
        h = h + gate * _mm(p_ref[rs, :].astype(_BF16), w_proj_ref[...])
        o_ref[rs, :] = _rmsnorm(h, nfin_ref[...]) if final else h


def _lru_kernel(layer, j, cast, x_ref, h0_ref, past_ref, g_ref, w_in_ref, cw_ref, cb_ref, wa_ref, ba_ref,
                wx_ref, bx_ref, lam_ref, w_out_ref, o_ref, hlast_ref, tail_ref, *rest):
    if cast:
        w_in_bf, wa_bf, wx_bf, w_out_bf, tail_s, xb_s, gate_s, a_s, b_s, h_s = rest
        w_in_bf[...] = w_in_ref[...].astype(_BF16)
        wa_bf[...] = (0.5 * wa_ref[...]).astype(_BF16)
        wx_bf[...] = (0.5 * wx_ref[...]).astype(_BF16)
        w_out_bf[...] = w_out_ref[...].astype(_BF16)
        w_in_ref, wa_ref, wx_ref, w_out_ref = w_in_bf, wa_bf, wx_bf, w_out_bf
    else:
        tail_s, xb_s, gate_s, a_s, b_s, h_s = rest
    @pl.when(pl.program_id(0) == 0)
    def _():
        hlast_ref[...] = h0_ref[...]
        tail_s[...] = _to_tm(past_ref[...])

    heads, dh, _ = wa_ref.shape
    width = heads * dh
    bsz, tt, _ = x_ref.shape
    n_blk = _LRU_BLOCKS if tt % (_LRU_BLOCKS * _SUBLANES) == 0 else 1
    tb = tt // n_blk
    for r in range(n_blk):
        rs = slice(r * tb * bsz, (r + 1) * tb * bsz)
        x = _to_tm(x_ref[:, r * tb:(r + 1) * tb, :])
        xn = _rmsnorm(x, _vec(g_ref, layer)).astype(_BF16)
        xb_s[rs, :] = _mm(xn, w_in_ref[:, width:])
        gate_s[rs, :] = _mm(xn, w_in_ref[:, :width])
    half_decay = (0.5 * _LRU_C) * _softplus(-_vec(lam_ref, j))
    xc, tail = _conv_tm(xb_s[...], tail_s[...], cw_ref[...], _vec(cb_ref, j))
    tail_s[...] = tail
    tail_ref[...] = _from_tm(tail, tail_ref.shape[1])
    for hh in range(heads):
        sl = slice(hh * dh, (hh + 1) * dh)
        a, b = _lru_coeffs(xc[:, sl], wa_ref[hh], ba_ref[j:j + 1, sl], wx_ref[hh], bx_ref[j:j + 1, sl],
                           half_decay[:, sl])
        a_s[:, sl] = a
        b_s[:, sl] = b
    hlast_ref[...] = _scan_slabs(a_s, b_s, h_s, hlast_ref[...])
    for r in range(n_blk):
        rs = slice(r * tb * bsz, (r + 1) * tb * bsz)
        y = _mm((_gelu(gate_s[rs, :]) * h_s[rs, :]).astype(_BF16), w_out_ref[...])
        if len(o_ref.shape) == 3:
            ts = slice(r * tb, (r + 1) * tb)
            o_ref[:, ts, :] = x_ref[:, ts, :] + _from_tm(y, tb)
        else:
            o_ref[rs, :] = _to_tm(x_ref[:, r * tb:(r + 1) * tb, :]) + y


def _gmlp_seq_kernel(layer, j, x_ref, g_ref, w_in_ref, lng_ref, lnb_ref, ws_ref, bs_ref, w_out_ref,
                     o_ref, s_s):
    rows = x_ref.shape[0]
    groups, chunk, _ = ws_ref.shape
    width = s_s.shape[1]
    gd = width // groups
    tri = (lax.broadcasted_iota(jnp.int32, (chunk, chunk), 0)
           >= lax.broadcasted_iota(jnp.int32, (chunk, chunk), 1))
    n_blk = _GMLP_BLOCKS if rows % (_GMLP_BLOCKS * chunk) == 0 else 1
    hr = rows // n_blk
    proj = []
    for h in range(n_blk):
        xn = _rmsnorm(x_ref[h * hr:(h + 1) * hr, :], _vec(g_ref, layer)).astype(_BF16)
        proj.append((_mm(xn, w_in_ref[:, width:]), _mm(xn, w_in_ref[:, :width])))
    n_c = hr // chunk
    for h in range(n_blk):
        zv, zu = proj[h]
        v = _layernorm(_gelu(zv), _vec(lng_ref, j), _vec(lnb_ref, j)).astype(_BF16)
        for g in range(groups):
            wg = jnp.where(tri, ws_ref[g], 0.0).astype(_BF16)
            cs = slice(g * gd, (g + 1) * gd)
            vg = jnp.concatenate([v[c * chunk:(c + 1) * chunk, cs] for c in range(n_c)], axis=1)
            sg = _mm(wg, vg)
            for c in range(n_c):
                s_s[h * hr + c * chunk:h * hr + (c + 1) * chunk, cs] = sg[:, c * gd:(c + 1) * gd] + bs_ref[:, cs]
        rs = slice(h * hr, (h + 1) * hr)
        o_ref[rs, :] = x_ref[rs, :] + _mm((_gelu(zu) * s_s[rs, :]).astype(_BF16), w_out_ref[...])


def _ffn_seq_kernel(layer, final, x_ref, p_ref, nf_ref, wg_ref, wu_ref, cw_ref, cb_ref, w_down_ref,
                    np_ref, w_gate_ref, w_proj_ref, nfin_ref, o_ref, tail_ref, prev_s, acc_s):
    @pl.when(pl.program_id(1) == 0)
    def _():
        prev_s[...] = jnp.zeros_like(prev_s)

    rows = x_ref.shape[0]
    keep = tail_ref.shape[0]

    def conv(cs, g, w, b):
        gc = _conv_rows(g, prev_s[:, cs], w, b)
        prev_s[:, cs] = g[rows - _SUBLANES:rows, :]
        tail_ref[:, cs] = g[rows - keep:rows, :]
        return gc

    _ffn_ple(layer, final, x_ref, p_ref, conv, nf_ref, wg_ref, wu_ref, cw_ref, cb_ref, w_down_ref,
             np_ref, w_gate_ref, w_proj_ref, nfin_ref, o_ref, acc_s)


def _gmlp_tm_kernel(layer, j, n_t, x_ref, g_ref, w_in_ref, lng_ref, lnb_ref, wsc_ref, bsc_ref, w_out_ref,
                    o_ref, v_ref, w_in_bf, w_out_bf, s_s):
    w_in_bf[...] = w_in_ref[...].astype(_BF16)
    w_out_bf[...] = w_out_ref[...].astype(_BF16)
    w_in_ref, w_out_ref = w_in_bf, w_out_bf
    x = x_ref[...]
    width = s_s.shape[1]
    bs = x.shape[0] // n_t
    z = _gelu(_mm(_rmsnorm(x, _vec(g_ref, layer)).astype(_BF16), w_in_ref[...]))
    u = z[:, :width]
    v = _layernorm(z[:, width:], _vec(lng_ref, j), _vec(lnb_ref, j))
    v_ref[...] = _from_tm(v, n_t)
    for t in range(n_t):
        s = bsc_ref[t:t + 1, :]
        for q in range(t + 1):
            s = s + wsc_ref[t, q:q + 1, :] * v[q * bs:(q + 1) * bs, :]
        s_s[t * bs:(t + 1) * bs, :] = s
    o_ref[...] = x + _mm((u * s_s[...]).astype(_BF16), w_out_ref[...])


def _ffn_tm_kernel(layer, final, x_ref, p_ref, past_ref, nf_ref, wg_ref, wu_ref, cw_ref, cb_ref, wd_ref,
                   np_ref, w_gate_ref, w_proj_ref, nfin_ref, o_ref, tail_ref, wg_bf_ref, wu_bf_ref,
                   wd_bf_ref, gate_bf_ref, proj_bf_ref, xn_s, acc_s):
    j = pl.program_id(0)
    n_keep = past_ref.shape[1]

    @pl.when(j == 0)
    def _():
        xn_s[...] = _rmsnorm(x_ref[...], _vec(nf_ref, layer)).astype(_BF16)
        acc_s[...] = jnp.zeros_like(acc_s)

    wg = wg_ref[...].astype(_BF16)
    wu = wu_ref[...].astype(_BF16)
    wd = wd_ref[...].astype(_BF16)
    wg_bf_ref[...] = wg
    wu_bf_ref[...] = wu
    wd_bf_ref[...] = wd
    xn = xn_s[...]
    g = _mm(xn, wg)
    u = _mm(xn, wu)
    gc, tail = _conv_tm(g, _to_tm(past_ref[...]), cw_ref[...], cb_ref[layer:layer + 1, :])
    tail_ref[...] = _from_tm(tail, n_keep)
    acc_s[...] += _mm((_gelu(gc) * u).astype(_BF16), wd)

    @pl.when(j == pl.num_programs(0) - 1)
    def _():
        w_gate = w_gate_ref[...].astype(_BF16)
        w_proj = w_proj_ref[...].astype(_BF16)
        gate_bf_ref[...] = w_gate
        proj_bf_ref[...] = w_proj
        h = x_ref[...] + acc_s[...]
        gate = _sigmoid(_mm(_rmsnorm(h, _vec(np_ref, layer)).astype(_BF16), w_gate))
        h = h + gate * _mm(_to_tm(p_ref[...]).astype(_BF16), w_proj)
        if final:
            o_ref[...] = _from_tm(_rmsnorm(h, nfin_ref[...]), o_ref.shape[1])
        else:
            o_ref[...] = h


def _whole(a):
    zeros = (0,) * a.ndim
    return pl.BlockSpec(a.shape, lambda *_: zeros, pipeline_mode=pl.Buffered(1))


def _layer_of(a, i):
    zeros = (0,) * (a.ndim - 1)
    return pl.BlockSpec((None,) + a.shape[1:], lambda *_: (i,) + zeros, pipeline_mode=pl.Buffered(1))


def _params(n_axes):
    return pltpu.CompilerParams(dimension_semantics=("arbitrary",) * n_axes, vmem_limit_bytes=_VMEM_LIMIT)


def _row_spec(tm, d):
    return pl.BlockSpec((None, tm, d), lambda b, t: (b, t, 0))


def _state_spec(shape):
    zeros = (0,) * len(shape)
    return pl.BlockSpec((None,) + shape, lambda b, t: (b,) + zeros)


def _full_spec(shape):
    zeros = (0,) * len(shape)
    return pl.BlockSpec(shape, lambda *_: zeros)


def _lru_specs(w, j, prompt):
    if prompt:
        mats = [_whole(a) for a in w["lru_bf16"][j]]
    else:
        mats = [_layer_of(w[k], j) for k in ("lru_w_in", "lru_w_a", "lru_w_x", "lru_w_out")]
    return [_whole(w["norm_mix"]), mats[0], _layer_of(w["lru_conv_w"], j), _whole(w["lru_conv_b"]),
            mats[1], _whole(w["lru_b_a"]), mats[2], _whole(w["lru_b_x"]), _whole(w["lru_lambda"]), mats[3]]


def _lru_args(w, j, prompt):
    mats = w["lru_bf16"][j] if prompt else [w[k] for k in ("lru_w_in", "lru_w_a", "lru_w_x", "lru_w_out")]
    return [w["norm_mix"], mats[0], w["lru_conv_w"], w["lru_conv_b"], mats[1], w["lru_b_a"], mats[2],
            w["lru_b_x"], w["lru_lambda"], mats[3]]


def _ffn_specs(w, i):
    wg, wu, wd, w_gate, w_proj = w["ffn_bf16"][i]
    return [_whole(w["norm_ffn"]), _whole(wg), _whole(wu), _layer_of(w["ffn_conv_w"], i),
            _whole(w["ffn_conv_b"]), _whole(wd), _whole(w["norm_ple"]), _whole(w_gate), _whole(w_proj),
            _whole(w["norm_final"])]


def _ffn_args(w, i):
    wg, wu, wd, w_gate, w_proj = w["ffn_bf16"][i]
    return [w["norm_ffn"], wg, wu, w["ffn_conv_w"], w["ffn_conv_b"], wd, w["norm_ple"], w_gate, w_proj,
            w["norm_final"]]


def _lru(x, h0, past, w, layer, j, prompt):
    bsz, t, d = x.shape
    width = w["lru_w_out"].shape[1]
    if prompt:
        tt = min(_LRU_TILE // bsz, t)
        assert tt % _SUBLANES == 0 and t % tt == 0 and bsz % _SUBLANES == 0
        o_shape, o_spec = x.shape, pl.BlockSpec((bsz, tt, d), lambda t: (0, t, 0))
        copies = []
    else:
        tt = t
        o_shape, o_spec = (t * bsz, d), _full_spec((t * bsz, d))
        copies = [jax.ShapeDtypeStruct(w[k].shape[1:], _BF16) for k in ("lru_w_in", "lru_w_a", "lru_w_x", "lru_w_out")]
    out_shape = [jax.ShapeDtypeStruct(o_shape, _F32), jax.ShapeDtypeStruct(h0.shape, _F32),
                 jax.ShapeDtypeStruct(past.shape, _F32)] + copies
    outs = pl.pallas_call(
        functools.partial(_lru_kernel, layer, j, not prompt),
        grid=(t // tt,),
        in_specs=[pl.BlockSpec((bsz, tt, d), lambda t: (0, t, 0)), _whole(h0), _whole(past)]
        + _lru_specs(w, j, prompt),
        out_specs=[o_spec, _full_spec(h0.shape), _full_spec(past.shape)] + [_full_spec(c.shape) for c in copies],
        out_shape=out_shape,
        scratch_shapes=[pltpu.VMEM((past.shape[1] * bsz, width), _F32)]
        + [pltpu.VMEM((bsz * tt, width), _F32)] * 5,
        compiler_params=_params(1),
        name="lru_seq" if prompt else "lru_tm",
    )(x, h0, past, *_lru_args(w, j, prompt))
    return outs[0], outs[1], outs[2], tuple(outs[3:])


def _gmlp_seq(x, w, layer, j):
    bsz, t, d = x.shape
    w_in, w_out = w["gm_bf16"][j]
    width = w_out.shape[0]
    tm = min(_GMLP_TILE, t)
    return pl.pallas_call(
        functools.partial(_gmlp_seq_kernel, layer, j),
        grid=(bsz, t // tm),
        in_specs=[_row_spec(tm, d), _whole(w["norm_mix"]), _whole(w_in), _whole(w["gm_ln_g"]),
                  _whole(w["gm_ln_b"]), _layer_of(w["gm_w_s"], j), _layer_of(w["gm_b_s_rows"], j),
                  _whole(w_out)],
        out_specs=_row_spec(tm, d),
        out_shape=jax.ShapeDtypeStruct((bsz, t, d), _F32),
        scratch_shapes=[pltpu.VMEM((tm, width), _F32)],
        compiler_params=_params(2),
        name="gmlp_seq",
    )(x, w["norm_mix"], w_in, w["gm_ln_g"], w["gm_ln_b"], w["gm_w_s"], w["gm_b_s_rows"], w_out)


def _ffn_seq(x, p_all, w, layer, final, n_keep):
    bsz, t, d = x.shape
    pdim = p_all.shape[-1]
    d_ff = w["ffn_conv_w"].shape[2]
    tm = min(_FFN_TILE, t)
    p_spec = pl.BlockSpec((None, None, tm, pdim), lambda b, t: (layer, b, t, 0))
    return pl.pallas_call(
        functools.partial(_ffn_seq_kernel, layer, final),
        grid=(bsz, t // tm),
        in_specs=[_row_spec(tm, d), p_spec] + _ffn_specs(w, layer),
        out_specs=[_row_spec(tm, d), _state_spec((n_keep, d_ff))],
        out_shape=[jax.ShapeDtypeStruct((bsz, t, d), _F32),
                   jax.ShapeDtypeStruct((bsz, n_keep, d_ff), _F32)],
        scratch_shapes=[pltpu.VMEM((_SUBLANES, d_ff), _F32), pltpu.VMEM((tm, d), _F32)],
        compiler_params=_params(2),
        name="ffn_seq",
    )(x, p_all, *_ffn_args(w, layer))


def _gmlp_tm(x2d, n_t, w, layer, j):
    rows, d = x2d.shape
    w_in, w_out = w["gm_w_in"], w["gm_w_out"]
    width = w_out.shape[1]
    out_shape = [jax.ShapeDtypeStruct((rows, d), _F32), jax.ShapeDtypeStruct((rows // n_t, n_t, width), _F32),
                 jax.ShapeDtypeStruct(w_in.shape[1:], _BF16), jax.ShapeDtypeStruct(w_out.shape[1:], _BF16)]
    outs = pl.pallas_call(
        functools.partial(_gmlp_tm_kernel, layer, j, n_t),
        grid=(1,),
        in_specs=[_full_spec(x2d.shape), _whole(w["norm_mix"]), _layer_of(w_in, j), _whole(w["gm_ln_g"]),
                  _whole(w["gm_ln_b"]), _layer_of(w["gm_w_s_head"], j), _layer_of(w["gm_b_s_head"], j),
                  _layer_of(w_out, j)],
        out_specs=[_full_spec(s.shape) for s in out_shape],
        out_shape=out_shape,
        scratch_shapes=[pltpu.VMEM((rows, width), _F32)],
        compiler_params=_params(1),
        name="gmlp_tm",
    )(x2d, w["norm_mix"], w_in, w["gm_ln_g"], w["gm_ln_b"], w["gm_w_s_head"], w["gm_b_s_head"], w_out)
    return outs[0], outs[1], tuple(outs[2:])


def _ffn_tm(x2d, p_all, past_all, w, layer, final):
    rows, d = x2d.shape
    _, bsz, n_t, pdim = p_all.shape
    n_keep, d_ff = past_all.shape[2], past_all.shape[3]
    w_up, w_down, w_gate, w_proj = w["ffn_w_up"], w["ffn_w_down"], w["ple_w_gate"], w["ple_w_proj"]
    kw = w["ffn_conv_w"].shape[1]
    n_layers = w["ffn_conv_b"].shape[0]
    cn = min(_FFN_TM_CHUNK, d_ff)
    n_chunks = d_ff // cn
    assert d_ff % cn == 0
    o_shape = (bsz, n_t, d) if final else (rows, d)
    out_shape = [jax.ShapeDtypeStruct(o_shape, _F32), jax.ShapeDtypeStruct(past_all.shape[1:], _F32),
                 jax.ShapeDtypeStruct((d, d_ff), _BF16), jax.ShapeDtypeStruct((d, d_ff), _BF16),
                 jax.ShapeDtypeStruct((d_ff, d), _BF16), jax.ShapeDtypeStruct(w_gate.shape[1:], _BF16),
                 jax.ShapeDtypeStruct(w_proj.shape[1:], _BF16)]
    in_specs = [
        _full_spec(x2d.shape), _layer_of(p_all, layer),
        pl.BlockSpec((None, bsz, n_keep, cn), lambda j: (layer, 0, 0, j)),
        _whole(w["norm_ffn"]),
        pl.BlockSpec((None, d, cn), lambda j: (layer, 0, j)),
        pl.BlockSpec((None, d, cn), lambda j: (layer, 0, n_chunks + j)),
        pl.BlockSpec((None, kw, cn), lambda j: (layer, 0, j)),
        pl.BlockSpec((n_layers, cn), lambda j: (0, j)),
        pl.BlockSpec((None, cn, d), lambda j: (layer, j, 0)),
        _whole(w["norm_ple"]), _layer_of(w_gate, layer), _layer_of(w_proj, layer), _whole(w["norm_final"]),
    ]
    out_specs = [
        _full_spec(o_shape), pl.BlockSpec((bsz, n_keep, cn), lambda j: (0, 0, j)),
        pl.BlockSpec((d, cn), lambda j: (0, j)), pl.BlockSpec((d, cn), lambda j: (0, j)),
        pl.BlockSpec((cn, d), lambda j: (j, 0)), _full_spec(w_gate.shape[1:]), _full_spec(w_proj.shape[1:]),
    ]
    outs = pl.pallas_call(
        functools.partial(_ffn_tm_kernel, layer, final),
        grid=(n_chunks,),
        in_specs=in_specs,
        out_specs=out_specs,
        out_shape=out_shape,
        scratch_shapes=[pltpu.VMEM((rows, d), _BF16), pltpu.VMEM((rows, d), _F32)],
        compiler_params=_params(1),
        name="ffn_tm",
    )(x2d, p_all, past_all, w["norm_ffn"], w_up, w_up, w["ffn_conv_w"], w["ffn_conv_b"], w_down,
      w["norm_ple"], w_gate, w_proj, w["norm_final"])
    return outs[0], outs[1], tuple(outs[2:])


def kernel(x_prompt, x_sample, p_prompt, p_sample, state_lru_h, state_lru_conv, state_ffn_conv, norm_mix, norm_ffn, norm_ple, norm_final, lru_w_in, lru_conv_w, lru_conv_b, lru_w_a, lru_b_a, lru_w_x, lru_b_x, lru_lambda, lru_w_out, gm_w_in, gm_ln_g, gm_ln_b, gm_w_s, gm_b_s, gm_w_out, ffn_w_up, ffn_conv_w, ffn_conv_b, ffn_w_down, ple_w_gate, ple_w_proj):
    depth = norm_mix.shape[0]
    bsz, seq, d_model = x_prompt.shape
    dec_b, dec_t, _ = x_sample.shape
    d_ff = ffn_w_down.shape[1]
    groups, chunk = gm_w_s.shape[1], gm_w_s.shape[2]
    gd = gm_w_out.shape[1] // groups
    tm = min(_GMLP_TILE, seq)
    assert seq % tm == 0 and tm % chunk == 0 and tm % _SUBLANES == 0 and d_ff % min(_FFN_CHUNK, d_ff) == 0
    assert _PAST_LEN % chunk == 0 and dec_t <= chunk and dec_b % _SUBLANES == 0
    assert seq % min(_FFN_TILE, seq) == 0 and depth % 2 == 0

    w = dict(
        norm_mix=norm_mix, norm_ffn=norm_ffn, norm_ple=norm_ple, norm_final=norm_final.reshape(1, -1),
        lru_w_in=lru_w_in, lru_conv_w=lru_conv_w, lru_conv_b=lru_conv_b, lru_w_a=lru_w_a, lru_w_x=lru_w_x,
        lru_b_a=0.5 * lru_b_a, lru_b_x=0.5 * lru_b_x, lru_lambda=lru_lambda, lru_w_out=lru_w_out,
        gm_w_in=gm_w_in, gm_ln_g=gm_ln_g, gm_ln_b=gm_ln_b, gm_w_s=gm_w_s, gm_w_out=gm_w_out,
        gm_b_s_rows=jnp.repeat(jnp.transpose(gm_b_s, (0, 2, 1)), gd, axis=2),
        gm_w_s_head=jnp.repeat(jnp.transpose(gm_w_s[:, :, :dec_t, :dec_t], (0, 2, 3, 1)), gd, axis=3),
        gm_b_s_head=jnp.repeat(jnp.transpose(gm_b_s[:, :, :dec_t], (0, 2, 1)), gd, axis=2),
        ffn_w_up=ffn_w_up, ffn_conv_w=ffn_conv_w, ffn_conv_b=ffn_conv_b, ffn_w_down=ffn_w_down,
        ple_w_gate=ple_w_gate, ple_w_proj=ple_w_proj, lru_bf16={}, gm_bf16={}, ffn_bf16={},
    )

    hp = x_prompt
    hs = x_sample
    lru_h_p, lru_conv_p, ffn_conv_p = [], [], []
    lru_h_s, lru_conv_s, ffn_conv_s, gm_v_s = [], [], [], []
    for i in range(depth):
        j = i // 2
        if i % 2 == 0:
            kw = lru_conv_w.shape[1]
            width = lru_w_out.shape[1]
            if hs.ndim == 2:
                hs = _from_tm(hs, dec_t)
            hs, hlast, tail, w["lru_bf16"][j] = _lru(hs, state_lru_h[j], state_lru_conv[j], w, i, j, False)
            lru_h_s.append(hlast)
            lru_conv_s.append(tail)
            hp, hlast, tail, _ = _lru(hp, jnp.zeros((bsz, width), _F32), jnp.zeros((bsz, kw - 1, width), _F32),
                                      w, i, j, True)
            lru_h_p.append(hlast)
            lru_conv_p.append(tail)
        else:
            hs, v, w["gm_bf16"][j] = _gmlp_tm(hs, dec_t, w, i, j)
            gm_v_s.append(v)
            hp = _gmlp_seq(hp, w, i, j)
        kw = ffn_conv_w.shape[1]
        final = i == depth - 1
        hs, tail, w["ffn_bf16"][i] = _ffn_tm(hs, p_sample, state_ffn_conv, w, i, final)
        ffn_conv_s.append(tail)
        hp, tail = _ffn_seq(hp, p_prompt, w, i, final, kw - 1)
        ffn_conv_p.append(tail)

    def stack(xs):
        return xs[0][None] if len(xs) == 1 else jnp.stack(xs)

    return (hp, hs, stack(lru_h_p), stack(lru_conv_p), stack(ffn_conv_p),
            stack(lru_h_s), stack(lru_conv_s), stack(ffn_conv_s), stack(gm_v_s))
```

```python
import functools
import math

import jax
import jax.numpy as jnp
from jax import lax
from jax.experimental import pallas as pl
from jax.experimental.pallas import tpu as pltpu

_EPS = 1e-6
_LRU_C = 8.0
_PAST_LEN = 16384
_SUBLANES = 8
_LRU_TILE = 1024
_GMLP_TILE = 1024
_FFN_TILE = 1024
_FFN_CHUNK = 1024
_GMLP_BLOCKS = 4
_PLE_ROWS = 256
_LRU_BLOCKS = 4
_FFN_TM_CHUNK = 512
_VMEM_LIMIT = 56 * 1024 * 1024

_F32 = jnp.float32
_BF16 = jnp.bfloat16


def _mm(a, b):
    return jnp.dot(a, b, preferred_element_type=_F32)


def _vec(ref, i):
    return ref[i:i + 1, :]


def _to_tm(a):
    b, t, c = a.shape
    return jnp.swapaxes(a, 0, 1).reshape(t * b, c)


def _from_tm(a2d, n_t):
    rows, c = a2d.shape
    return jnp.swapaxes(a2d.reshape(n_t, rows // n_t, c), 0, 1)


def _rmsnorm(x, g):
    ms = jnp.mean(x * x, axis=-1, keepdims=True)
    return (x * lax.rsqrt(ms + _EPS)) * g


def _layernorm(x, g, b):
    mu = jnp.mean(x, axis=-1, keepdims=True)
    xc = x - mu
    y = xc * lax.rsqrt(jnp.mean(xc * xc, axis=-1, keepdims=True) + _EPS)
    return y * g + b


def _gelu(x):
    c = math.sqrt(2.0 / math.pi)
    return x * (0.5 * (1.0 + jnp.tanh(c * (x + 0.044715 * (x * x * x)))))


def _softplus(x):
    return jnp.maximum(x, 0.0) + jnp.log1p(jnp.exp(-jnp.abs(x)))


def _conv_rows(x, prev8, w, b):
    kw = w.shape[0]
    x0 = x[0:_SUBLANES]
    row = lax.broadcasted_iota(jnp.int32, x0.shape, 0)
    y = None
    y0 = None
    for k in range(kw):
        d = kw - 1 - k
        wk = w[k:k + 1]
        if d == 0:
            t, t0 = x * wk, x0 * wk
        else:
            t = pltpu.roll(x, d, 0) * wk
            t0 = jnp.where(row < d, pltpu.roll(prev8, d, 0), pltpu.roll(x0, d, 0)) * wk
        y = t if y is None else y + t
        y0 = t0 if y0 is None else y0 + t0
    return jnp.concatenate([y0, y[_SUBLANES:]], axis=0) + b


def _conv_tm(x, past, w, b):
    kw = w.shape[0]
    rows = x.shape[0]
    rb = past.shape[0] // (kw - 1)
    ext = jnp.concatenate([past, x], axis=0)
    y = ext[0:rows] * w[0:1]
    for k in range(1, kw):
        y = y + ext[k * rb:k * rb + rows] * w[k:k + 1]
    return y + b, ext[rows:]


def _sigmoid(z):
    return 0.5 * jnp.tanh(0.5 * z) + 0.5


def _lru_coeffs(xh, wa_half, ba_half, wx_half, bx_half, half_decay):
    xhb = xh.astype(_BF16)
    t_r = jnp.tanh(_mm(xhb, wa_half) + ba_half)
    t_i = jnp.tanh(_mm(xhb, wx_half) + bx_half)
    m = t_r * half_decay + half_decay
    a = jnp.exp(-m)
    u = jnp.tanh(m) * (1.0 + a * a)
    mult = jnp.where(u > 0.0, u * lax.rsqrt(u), 0.0)
    hx = 0.5 * xh
    return a, mult * (t_i * hx + hx)


def _ffn_ple(layer, final, x_ref, p_ref, conv_fn, nf_ref, wg_ref, wu_ref, cw_ref, cb_ref, w_down_ref,
             np_ref, w_gate_ref, w_proj_ref, nfin_ref, o_ref, acc_s):
    rows = x_ref.shape[0]
    d_ff = w_down_ref.shape[0]
    cn = min(_FFN_CHUNK, d_ff)
    rbk = min(_PLE_ROWS, rows)
    blocks = [slice(r * rbk, (r + 1) * rbk) for r in range(rows // rbk)]
    xn_blocks = [_rmsnorm(x_ref[rs, :], _vec(nf_ref, layer)).astype(_BF16) for rs in blocks]
    xn = jnp.concatenate(xn_blocks, axis=0)
    for j in range(d_ff // cn):
        cs = slice(j * cn, (j + 1) * cn)
        if j == 0:
            g = jnp.concatenate([_mm(xb, wg_ref[:, cs]) for xb in xn_blocks], axis=0)
            u = jnp.concatenate([_mm(xb, wu_ref[:, cs]) for xb in xn_blocks], axis=0)
        else:
            g = _mm(xn, wg_ref[:, cs])
            u = _mm(xn, wu_ref[:, cs])
        gc = conv_fn(cs, g, cw_ref[:, cs], cb_ref[layer:layer + 1, cs])
        d = _mm((_gelu(gc) * u).astype(_BF16), w_down_ref[cs, :])
        if j == 0:
            acc_s[...] = d
        else:
            acc_s[...] += d
    for rs in blocks:
        h = x_ref[rs, :] + acc_s[rs, :]
        gate = _sigmoid(_mm(_rmsnorm(h, _vec(np_ref, layer)).astype(_BF16), w_gate_ref[...]))
        h = h + gate * _mm(p_ref[rs, :].astype(_BF16), w_proj_ref[...])
        o_ref[rs, :] = _rmsnorm(h, nfin_ref[...]) if final else h


def _lru_kernel(layer, j, cast, x_ref, h0_ref, past_ref, g_ref, w_in_ref, cw_ref, cb_ref, wa_ref, ba_ref,
                wx_ref, bx_ref, lam_ref, w_out_ref, o_ref, hlast_ref, tail_ref, *rest):
    if cast:
        w_in_bf, wa_bf, wx_bf, w_out_bf, tail_s, xb_s, gate_s, a_s, b_s, h_s = rest
        w_in_bf[...] = w_in_ref[...].astype(_BF16)
        wa_bf[...] = (0.5 * wa_ref[...]).astype(_BF16)
        wx_bf[...] = (0.5 * wx_ref[...]).astype(_BF16)
        w_out_bf[...] = w_out_ref[...].astype(_BF16)
        w_in_ref, wa_ref, wx_ref, w_out_ref = w_in_bf, wa_bf, wx_bf, w_out_bf
    else:
        tail_s, xb_s, gate_s, a_s, b_s, h_s = rest
    @pl.when(pl.program_id(0) == 0)
    def _():
        hlast_ref[...] = h0_ref[...]
        tail_s[...] = _to_tm(past_ref[...])

    heads, dh, _ = wa_ref.shape
    width = heads * dh
    bsz, tt, _ = x_ref.shape
    n_blk = _LRU_BLOCKS if tt % (_LRU_BLOCKS * _SUBLANES) == 0 else 1
    tb = tt // n_blk
    for r in range(n_blk):
        rs = slice(r * tb * bsz, (r + 1) * tb * bsz)
        x = _to_tm(x_ref[:, r * tb:(r + 1) * tb, :])
        xn = _rmsnorm(x, _vec(g_ref, layer)).astype(_BF16)
        xb_s[rs, :] = _mm(xn, w_in_ref[:, width:])
        gate_s[rs, :] = _mm(xn, w_in_ref[:, :width])
    half_decay = (0.5 * _LRU_C) * _softplus(-_vec(lam_ref, j))
    tail = tail_s[...]
    h = hlast_ref[...]
    for r in range(n_blk):
        r0 = r * tb * bsz
        rs = slice(r0, r0 + tb * bsz)
        xc, tail = _conv_tm(xb_s[rs, :], tail, cw_ref[...], _vec(cb_ref, j))
        for hh in range(heads):
            sl = slice(hh * dh, (hh + 1) * dh)
            a, b = _lru_coeffs(xc[:, sl], wa_ref[hh], ba_ref[j:j + 1, sl], wx_ref[hh], bx_ref[j:j + 1, sl],
                               half_decay[:, sl])
            a_s[rs, sl] = a
            b_s[rs, sl] = b
        for t in range(tb):
            ts = slice(r0 + t * bsz, r0 + (t + 1) * bsz)
            h = a_s[ts, :] * h + b_s[ts, :]
            h_s[ts, :] = h
        y = _mm((_gelu(gate_s[rs, :]) * h_s[rs, :]).astype(_BF16), w_out_ref[...])
        if len(o_ref.shape) == 3:
            tsl = slice(r * tb, (r + 1) * tb)
            o_ref[:, tsl, :] = x_ref[:, tsl, :] + _from_tm(y, tb)
        else:
            o_ref[rs, :] = _to_tm(x_ref[:, r * tb:(r + 1) * tb, :]) + y
    tail_s[...] = tail
    tail_ref[...] = _from_tm(tail, tail_ref.shape[1])
    hlast_ref[...] = h


def _gmlp_seq_kernel(layer, j, x_ref, g_ref, w_in_ref, lng_ref, lnb_ref, ws_ref, bs_ref, w_out_ref,
                     o_ref, s_s):
    rows = x_ref.shape[0]
    groups, chunk, _ = ws_ref.shape
    width = s_s.shape[1]
    gd = width // groups
    tri = (lax.broadcasted_iota(jnp.int32, (chunk, chunk), 0)
           >= lax.broadcasted_iota(jnp.int32, (chunk, chunk), 1))
    n_blk = _GMLP_BLOCKS if rows % (_GMLP_BLOCKS * chunk) == 0 else 1
    hr = rows // n_blk
    proj = []
    for h in range(n_blk):
        xn = _rmsnorm(x_ref[h * hr:(h + 1) * hr, :], _vec(g_ref, layer)).astype(_BF16)
        proj.append((_mm(xn, w_in_ref[:, width:]), _mm(xn, w_in_ref[:, :width])))
    n_c = hr // chunk
    for h in range(n_blk):
        zv, zu = proj[h]
        v = _layernorm(_gelu(zv), _vec(lng_ref, j), _vec(lnb_ref, j)).astype(_BF16)
        for g in range(groups):
            wg = jnp.where(tri, ws_ref[g], 0.0).astype(_BF16)
            cs = slice(g * gd, (g + 1) * gd)
            vg = jnp.concatenate([v[c * chunk:(c + 1) * chunk, cs] for c in range(n_c)], axis=1)
            sg = _mm(wg, vg)
            for c in range(n_c):
                s_s[h * hr + c * chunk:h * hr + (c + 1) * chunk, cs] = sg[:, c * gd:(c + 1) * gd] + bs_ref[:, cs]
        rs = slice(h * hr, (h + 1) * hr)
        o_ref[rs, :] = x_ref[rs, :] + _mm((_gelu(zu) * s_s[rs, :]).astype(_BF16), w_out_ref[...])


def _ffn_seq_kernel(layer, final, x_ref, p_ref, nf_ref, wg_ref, wu_ref, cw_ref, cb_ref, w_down_ref,
                    np_ref, w_gate_ref, w_proj_ref, nfin_ref, o_ref, tail_ref, prev_s, acc_s):
    @pl.when(pl.program_id(1) == 0)
    def _():
        prev_s[...] = jnp.zeros_like(prev_s)

    rows = x_ref.shape[0]
    keep = tail_ref.shape[0]

    def conv(cs, g, w, b):
        gc = _conv_rows(g, prev_s[:, cs], w, b)
        prev_s[:, cs] = g[rows - _SUBLANES:rows, :]
        tail_ref[:, cs] = g[rows - keep:rows, :]
        return gc

    _ffn_ple(layer, final, x_ref, p_ref, conv, nf_ref, wg_ref, wu_ref, cw_ref, cb_ref, w_down_ref,
             np_ref, w_gate_ref, w_proj_ref, nfin_ref, o_ref, acc_s)


def _gmlp_tm_kernel(layer, j, n_t, x_ref, g_ref, w_in_ref, lng_ref, lnb_ref, wsc_ref, bsc_ref, w_out_ref,
                    o_ref, v_ref, w_in_bf, w_out_bf, s_s):
    w_in_bf[...] = w_in_ref[...].astype(_BF16)
    w_out_bf[...] = w_out_ref[...].astype(_BF16)
    w_in_ref, w_out_ref = w_in_bf, w_out_bf
    x = x_ref[...]
    width = s_s.shape[1]
    bs = x.shape[0] // n_t
    z = _gelu(_mm(_rmsnorm(x, _vec(g_ref, layer)).astype(_BF16), w_in_ref[...]))
    u = z[:, :width]
    v = _layernorm(z[:, width:], _vec(lng_ref, j), _vec(lnb_ref, j))
    v_ref[...] = _from_tm(v, n_t)
    for t in range(n_t):
        s = bsc_ref[t:t + 1, :]
        for q in range(t + 1):
            s = s + wsc_ref[t, q:q + 1, :] * v[q * bs:(q + 1) * bs, :]
        s_s[t * bs:(t + 1) * bs, :] = s
    o_ref[...] = x + _mm((u * s_s[...]).astype(_BF16), w_out_ref[...])


def _ffn_tm_kernel(layer, final, x_ref, p_ref, past_ref, nf_ref, wg_ref, wu_ref, cw_ref, cb_ref, wd_ref,
                   np_ref, w_gate_ref, w_proj_ref, nfin_ref, o_ref, tail_ref, wg_bf_ref, wu_bf_ref,
                   wd_bf_ref, gate_bf_ref, proj_bf_ref, xn_s, acc_s):
    j = pl.program_id(0)
    n_keep = past_ref.shape[1]

    @pl.when(j == 0)
    def _():
        xn_s[...] = _rmsnorm(x_ref[...], _vec(nf_ref, layer)).astype(_BF16)
        acc_s[...] = jnp.zeros_like(acc_s)

    wg = wg_ref[...].astype(_BF16)
    wu = wu_ref[...].astype(_BF16)
    wd = wd_ref[...].astype(_BF16)
    wg_bf_ref[...] = wg
    wu_bf_ref[...] = wu
    wd_bf_ref[...] = wd
    xn = xn_s[...]
    g = _mm(xn, wg)
    u = _mm(xn, wu)
    gc, tail = _conv_tm(g, _to_tm(past_ref[...]), cw_ref[...], cb_ref[layer:layer + 1, :])
    tail_ref[...] = _from_tm(tail, n_keep)
    acc_s[...] += _mm((_gelu(gc) * u).astype(_BF16), wd)

    @pl.when(j == pl.num_programs(0) - 1)
    def _():
        w_gate = w_gate_ref[...].astype(_BF16)
        w_proj = w_proj_ref[...].astype(_BF16)
        gate_bf_ref[...] = w_gate
        proj_bf_ref[...] = w_proj
        h = x_ref[...] + acc_s[...]
        gate = _sigmoid(_mm(_rmsnorm(h, _vec(np_ref, layer)).astype(_BF16), w_gate))
        h = h + gate * _mm(_to_tm(p_ref[...]).astype(_BF16), w_proj)
        if final:
            o_ref[...] = _from_tm(_rmsnorm(h, nfin_ref[...]), o_ref.shape[1])
        else:
            o_ref[...] = h


def _whole(a):
    zeros = (0,) * a.ndim
    return pl.BlockSpec(a.shape, lambda *_: zeros, pipeline_mode=pl.Buffered(1))


def _layer_of(a, i):
    zeros = (0,) * (a.ndim - 1)
    return pl.BlockSpec((None,) + a.shape[1:], lambda *_: (i,) + zeros, pipeline_mode=pl.Buffered(1))


def _params(n_axes):
    return pltpu.CompilerParams(dimension_semantics=("arbitrary",) * n_axes, vmem_limit_bytes=_VMEM_LIMIT)


def _row_spec(tm, d):
    return pl.BlockSpec((None, tm, d), lambda b, t: (b, t, 0))


def _state_spec(shape):
    zeros = (0,) * len(shape)
    return pl.BlockSpec((None,) + shape, lambda b, t: (b,) + zeros)


def _full_spec(shape):
    zeros = (0,) * len(shape)
    return pl.BlockSpec(shape, lambda *_: zeros)


def _lru_specs(w, j, prompt):
    if prompt:
        mats = [_whole(a) for a in w["lru_bf16"][j]]
    else:
        mats = [_layer_of(w[k], j) for k in ("lru_w_in", "lru_w_a", "lru_w_x", "lru_w_out")]
    return [_whole(w["norm_mix"]), mats[0], _layer_of(w["lru_conv_w"], j), _whole(w["lru_conv_b"]),
            mats[1], _whole(w["lru_b_a"]), mats[2], _whole(w["lru_b_x"]), _whole(w["lru_lambda"]), mats[3]]


def _lru_args(w, j, prompt):
    mats = w["lru_bf16"][j] if prompt else [w[k] for k in ("lru_w_in", "lru_w_a", "lru_w_x", "lru_w_out")]
    return [w["norm_mix"], mats[0], w["lru_conv_w"], w["lru_conv_b"], mats[1], w["lru_b_a"], mats[2],
            w["lru_b_x"], w["lru_lambda"], mats[3]]


def _ffn_specs(w, i):
    wg, wu, wd, w_gate, w_proj = w["ffn_bf16"][i]
    return [_whole(w["norm_ffn"]), _whole(wg), _whole(wu), _layer_of(w["ffn_conv_w"], i),
            _whole(w["ffn_conv_b"]), _whole(wd), _whole(w["norm_ple"]), _whole(w_gate), _whole(w_proj),
            _whole(w["norm_final"])]


def _ffn_args(w, i):
    wg, wu, wd, w_gate, w_proj = w["ffn_bf16"][i]
    return [w["norm_ffn"], wg, wu, w["ffn_conv_w"], w["ffn_conv_b"], wd, w["norm_ple"], w_gate, w_proj,
            w["norm_final"]]


def _lru(x, h0, past, w, layer, j, prompt):
    bsz, t, d = x.shape
    width = w["lru_w_out"].shape[1]
    if prompt:
        tt = min(_LRU_TILE // bsz, t)
        assert tt % _SUBLANES == 0 and t % tt == 0 and bsz % _SUBLANES == 0
        o_shape, o_spec = x.shape, pl.BlockSpec((bsz, tt, d), lambda t: (0, t, 0))
        copies = []
    else:
        tt = t
        o_shape, o_spec = (t * bsz, d), _full_spec((t * bsz, d))
        copies = [jax.ShapeDtypeStruct(w[k].shape[1:], _BF16) for k in ("lru_w_in", "lru_w_a", "lru_w_x", "lru_w_out")]
    out_shape = [jax.ShapeDtypeStruct(o_shape, _F32), jax.ShapeDtypeStruct(h0.shape, _F32),
                 jax.ShapeDtypeStruct(past.shape, _F32)] + copies
    outs = pl.pallas_call(
        functools.partial(_lru_kernel, layer, j, not prompt),
        grid=(t // tt,),
        in_specs=[pl.BlockSpec((bsz, tt, d), lambda t: (0, t, 0)), _whole(h0), _whole(past)]
        + _lru_specs(w, j, prompt),
        out_specs=[o_spec, _full_spec(h0.shape), _full_spec(past.shape)] + [_full_spec(c.shape) for c in copies],
        out_shape=out_shape,
        scratch_shapes=[pltpu.VMEM((past.shape[1] * bsz, width), _F32)]
        + [pltpu.VMEM((bsz * tt, width), _F32)] * 5,
        compiler_params=_params(1),
        name="lru_seq" if prompt else "lru_tm",
    )(x, h0, past, *_lru_args(w, j, prompt))
    return outs[0], outs[1], outs[2], tuple(outs[3:])


def _gmlp_seq(x, w, layer, j):
    bsz, t, d = x.shape
    w_in, w_out = w["gm_bf16"][j]
    width = w_out.shape[0]
    tm = min(_GMLP_TILE, t)
    return pl.pallas_call(
        functools.partial(_gmlp_seq_kernel, layer, j),
        grid=(bsz, t // tm),
        in_specs=[_row_spec(tm, d), _whole(w["norm_mix"]), _whole(w_in), _whole(w["gm_ln_g"]),
                  _whole(w["gm_ln_b"]), _layer_of(w["gm_w_s"], j), _layer_of(w["gm_b_s_rows"], j),
                  _whole(w_out)],
        out_specs=_row_spec(tm, d),
        out_shape=jax.ShapeDtypeStruct((bsz, t, d), _F32),
        scratch_shapes=[pltpu.VMEM((tm, width), _F32)],
        compiler_params=_params(2),
        name="gmlp_seq",
    )(x, w["norm_mix"], w_in, w["gm_ln_g"], w["gm_ln_b"], w["gm_w_s"], w["gm_b_s_rows"], w_out)


def _ffn_seq(x, p_all, w, layer, final, n_keep):
    bsz, t, d = x.shape
    pdim = p_all.shape[-1]
    d_ff = w["ffn_conv_w"].shape[2]
    tm = min(_FFN_TILE, t)
    p_spec = pl.BlockSpec((None, None, tm, pdim), lambda b, t: (layer, b, t, 0))
    return pl.pallas_call(
        functools.partial(_ffn_seq_kernel, layer, final),
        grid=(bsz, t // tm),
        in_specs=[_row_spec(tm, d), p_spec] + _ffn_specs(w, layer),
        out_specs=[_row_spec(tm, d), _state_spec((n_keep, d_ff))],
        out_shape=[jax.ShapeDtypeStruct((bsz, t, d), _F32),
                   jax.ShapeDtypeStruct((bsz, n_keep, d_ff), _F32)],
        scratch_shapes=[pltpu.VMEM((_SUBLANES, d_ff), _F32), pltpu.VMEM((tm, d), _F32)],
        compiler_params=_params(2),
        name="ffn_seq",
    )(x, p_all, *_ffn_args(w, layer))


def _gmlp_tm(x2d, n_t, w, layer, j):
    rows, d = x2d.shape
    w_in, w_out = w["gm_w_in"], w["gm_w_out"]
    width = w_out.shape[1]
    out_shape = [jax.ShapeDtypeStruct((rows, d), _F32), jax.ShapeDtypeStruct((rows // n_t, n_t, width), _F32),
                 jax.ShapeDtypeStruct(w_in.shape[1:], _BF16), jax.ShapeDtypeStruct(w_out.shape[1:], _BF16)]
    outs = pl.pallas_call(
        functools.partial(_gmlp_tm_kernel, layer, j, n_t),
        grid=(1,),
        in_specs=[_full_spec(x2d.shape), _whole(w["norm_mix"]), _layer_of(w_in, j), _whole(w["gm_ln_g"]),
                  _whole(w["gm_ln_b"]), _layer_of(w["gm_w_s_head"], j), _layer_of(w["gm_b_s_head"], j),
                  _layer_of(w_out, j)],
        out_specs=[_full_spec(s.shape) for s in out_shape],
        out_shape=out_shape,
        scratch_shapes=[pltpu.VMEM((rows, width), _F32)],
        compiler_params=_params(1),
        name="gmlp_tm",
    )(x2d, w["norm_mix"], w_in, w["gm_ln_g"], w["gm_ln_b"], w["gm_w_s_head"], w["gm_b_s_head"], w_out)
    return outs[0], outs[1], tuple(outs[2:])


def _ffn_tm(x2d, p_all, past_all, w, layer, final):
    rows, d = x2d.shape
    _, bsz, n_t, pdim = p_all.shape
    n_keep, d_ff = past_all.shape[2], past_all.shape[3]
    w_up, w_down, w_gate, w_proj = w["ffn_w_up"], w["ffn_w_down"], w["ple_w_gate"], w["ple_w_proj"]
    kw = w["ffn_conv_w"].shape[1]
    n_layers = w["ffn_conv_b"].shape[0]
    cn = min(_FFN_TM_CHUNK, d_ff)
    n_chunks = d_ff // cn
    assert d_ff % cn == 0
    o_shape = (bsz, n_t, d) if final else (rows, d)
    out_shape = [jax.ShapeDtypeStruct(o_shape, _F32), jax.ShapeDtypeStruct(past_all.shape[1:], _F32),
                 jax.ShapeDtypeStruct((d, d_ff), _BF16), jax.ShapeDtypeStruct((d, d_ff), _BF16),
                 jax.ShapeDtypeStruct((d_ff, d), _BF16), jax.ShapeDtypeStruct(w_gate.shape[1:], _BF16),
                 jax.ShapeDtypeStruct(w_proj.shape[1:], _BF16)]
    in_specs = [
        _full_spec(x2d.shape), _layer_of(p_all, layer),
        pl.BlockSpec((None, bsz, n_keep, cn), lambda j: (layer, 0, 0, j)),
        _whole(w["norm_ffn"]),
        pl.BlockSpec((None, d, cn), lambda j: (layer, 0, j)),
        pl.BlockSpec((None, d, cn), lambda j: (layer, 0, n_chunks + j)),
        pl.BlockSpec((None, kw, cn), lambda j: (layer, 0, j)),
        pl.BlockSpec((n_layers, cn), lambda j: (0, j)),
        pl.BlockSpec((None, cn, d), lambda j: (layer, j, 0)),
        _whole(w["norm_ple"]), _layer_of(w_gate, layer), _layer_of(w_proj, layer), _whole(w["norm_final"]),
    ]
    out_specs = [
        _full_spec(o_shape), pl.BlockSpec((bsz, n_keep, cn), lambda j: (0, 0, j)),
        pl.BlockSpec((d, cn), lambda j: (0, j)), pl.BlockSpec((d, cn), lambda j: (0, j)),
        pl.BlockSpec((cn, d), lambda j: (j, 0)), _full_spec(w_gate.shape[1:]), _full_spec(w_proj.shape[1:]),
    ]
    outs = pl.pallas_call(
        functools.partial(_ffn_tm_kernel, layer, final),
        grid=(n_chunks,),
        in_specs=in_specs,
        out_specs=out_specs,
        out_shape=out_shape,
        scratch_shapes=[pltpu.VMEM((rows, d), _BF16), pltpu.VMEM((rows, d), _F32)],
        compiler_params=_params(1),
        name="ffn_tm",
    )(x2d, p_all, past_all, w["norm_ffn"], w_up, w_up, w["ffn_conv_w"], w["ffn_conv_b"], w_down,
      w["norm_ple"], w_gate, w_proj, w["norm_final"])
    return outs[0], outs[1], tuple(outs[2:])


def kernel(x_prompt, x_sample, p_prompt, p_sample, state_lru_h, state_lru_conv, state_ffn_conv, norm_mix, norm_ffn, norm_ple, norm_final, lru_w_in, lru_conv_w, lru_conv_b, lru_w_a, lru_b_a, lru_w_x, lru_b_x, lru_lambda, lru_w_out, gm_w_in, gm_ln_g, gm_ln_b, gm_w_s, gm_b_s, gm_w_out, ffn_w_up, ffn_conv_w, ffn_conv_b, ffn_w_down, ple_w_gate, ple_w_proj):
    depth = norm_mix.shape[0]
    bsz, seq, d_model = x_prompt.shape
    dec_b, dec_t, _ = x_sample.shape
    d_ff = ffn_w_down.shape[1]
    groups, chunk = gm_w_s.shape[1], gm_w_s.shape[2]
    gd = gm_w_out.shape[1] // groups
    tm = min(_GMLP_TILE, seq)
    assert seq % tm == 0 and tm % chunk == 0 and tm % _SUBLANES == 0 and d_ff % min(_FFN_CHUNK, d_ff) == 0
    assert _PAST_LEN % chunk == 0 and dec_t <= chunk and dec_b % _SUBLANES == 0
    assert seq % min(_FFN_TILE, seq) == 0 and depth % 2 == 0

    w = dict(
        norm_mix=norm_mix, norm_ffn=norm_ffn, norm_ple=norm_ple, norm_final=norm_final.reshape(1, -1),
        lru_w_in=lru_w_in, lru_conv_w=lru_conv_w, lru_conv_b=lru_conv_b, lru_w_a=lru_w_a, lru_w_x=lru_w_x,
        lru_b_a=0.5 * lru_b_a, lru_b_x=0.5 * lru_b_x, lru_lambda=lru_lambda, lru_w_out=lru_w_out,
        gm_w_in=gm_w_in, gm_ln_g=gm_ln_g, gm_ln_b=gm_ln_b, gm_w_s=gm_w_s, gm_w_out=gm_w_out,
        gm_b_s_rows=jnp.repeat(jnp.transpose(gm_b_s, (0, 2, 1)), gd, axis=2),
        gm_w_s_head=jnp.repeat(jnp.transpose(gm_w_s[:, :, :dec_t, :dec_t], (0, 2, 3, 1)), gd, axis=3),
        gm_b_s_head=jnp.repeat(jnp.transpose(gm_b_s[:, :, :dec_t], (0, 2, 1)), gd, axis=2),
        ffn_w_up=ffn_w_up, ffn_conv_w=ffn_conv_w, ffn_conv_b=ffn_conv_b, ffn_w_down=ffn_w_down,
        ple_w_gate=ple_w_gate, ple_w_proj=ple_w_proj, lru_bf16={}, gm_bf16={}, ffn_bf16={},
    )

    hp = x_prompt
    hs = x_sample
    lru_h_p, lru_conv_p, ffn_conv_p = [], [], []
    lru_h_s, lru_conv_s, ffn_conv_s, gm_v_s = [], [], [], []
    for i in range(depth):
        j = i // 2
        if i % 2 == 0:
            kw = lru_conv_w.shape[1]
            width = lru_w_out.shape[1]
            if hs.ndim == 2:
                hs = _from_tm(hs, dec_t)
            hs, hlast, tail, w["lru_bf16"][j] = _lru(hs, state_lru_h[j], state_lru_conv[j], w, i, j, False)
            lru_h_s.append(hlast)
            lru_conv_s.append(tail)
            hp, hlast, tail, _ = _lru(hp, jnp.zeros((bsz, width), _F32), jnp.zeros((bsz, kw - 1, width), _F32),
                                      w, i, j, True)
            lru_h_p.append(hlast)
            lru_conv_p.append(tail)
        else:
            hs, v, w["gm_bf16"][j] = _gmlp_tm(hs, dec_t, w, i, j)
            gm_v_s.append(v)
            hp = _gmlp_seq(hp, w, i, j)
        kw = ffn_conv_w.shape[1]
        final = i == depth - 1
        hs, tail, w["ffn_bf16"][i] = _ffn_tm(hs, p_sample, state_ffn_conv, w, i, final)
        ffn_conv_s.append(tail)
        hp, tail = _ffn_seq(hp, p_prompt, w, i, final, kw - 1)
        ffn_conv_p.append(tail)

    def stack(xs):
        return xs[0][None] if len(xs) == 1 else jnp.stack(xs)

    return (hp, hs, stack(lru_h_p), stack(lru_conv_p), stack(ffn_conv_p),
            stack(lru_h_s), stack(lru_conv_s), stack(ffn_conv_s), stack(gm_v_s))
```

```python
import functools
import math

import jax
import jax.numpy as jnp
from jax import lax
from jax.experimental import pallas as pl
from jax.experimental.pallas import tpu as pltpu

_EPS = 1e-6
_LRU_C = 8.0
_PAST_LEN = 16384
_SUBLANES = 8
_LRU_TILE = 1024
_GMLP_TILE = 1024
_FFN_TILE = 1024
_FFN_CHUNK = 1024
_GMLP_BLOCKS = 4
_PLE_ROWS = 256
_LRU_BLOCKS = 4
_FFN_TM_CHUNK = 512
_VMEM_LIMIT = 56 * 1024 * 1024

_F32 = jnp.float32
_BF16 = jnp.bfloat16


def _mm(a, b):
    return jnp.dot(a, b, preferred_element_type=_F32)


def _vec(ref, i):
    return ref[i:i + 1, :]


def _to_tm(a):
    b, t, c = a.shape
    return jnp.swapaxes(a, 0, 1).reshape(t * b, c)


def _from_tm(a2d, n_t):
    rows, c = a2d.shape
    return jnp.swapaxes(a2d.reshape(n_t, rows // n_t, c), 0, 1)


def _rmsnorm(x, g):
    ms = jnp.mean(x * x, axis=-1, keepdims=True)
    return (x * lax.rsqrt(ms + _EPS)) * g


def _layernorm(x, g, b):
    mu = jnp.mean(x, axis=-1, keepdims=True)
    xc = x - mu
    y = xc * lax.rsqrt(jnp.mean(xc * xc, axis=-1, keepdims=True) + _EPS)
    return y * g + b


def _gelu(x):
    c = math.sqrt(2.0 / math.pi)
    return x * (0.5 * (1.0 + jnp.tanh(c * (x + 0.044715 * (x * x * x)))))


def _softplus(x):
    return jnp.maximum(x, 0.0) + jnp.log1p(jnp.exp(-jnp.abs(x)))


def _conv_rows(x, prev8, w, b):
    kw = w.shape[0]
    x0 = x[0:_SUBLANES]
    row = lax.broadcasted_iota(jnp.int32, x0.shape, 0)
    y = None
    y0 = None
    for k in range(kw):
        d = kw - 1 - k
        wk = w[k:k + 1]
        if d == 0:
            t, t0 = x * wk, x0 * wk
        else:
            t = pltpu.roll(x, d, 0) * wk
            t0 = jnp.where(row < d, pltpu.roll(prev8, d, 0), pltpu.roll(x0, d, 0)) * wk
        y = t if y is None else y + t
        y0 = t0 if y0 is None else y0 + t0
    return jnp.concatenate([y0, y[_SUBLANES:]], axis=0) + b


def _conv_tm(x, past, w, b):
    kw = w.shape[0]
    rows = x.shape[0]
    rb = past.shape[0] // (kw - 1)
    ext = jnp.concatenate([past, x], axis=0)
    y = ext[0:rows] * w[0:1]
    for k in range(1, kw):
        y = y + ext[k * rb:k * rb + rows] * w[k:k + 1]
    return y + b, ext[rows:]


def _sigmoid(z):
    return 0.5 * jnp.tanh(0.5 * z) + 0.5


def _lru_coeffs(xh, wa_half, ba_half, wx_half, bx_half, half_decay):
    xhb = xh.astype(_BF16)
    t_r = jnp.tanh(_mm(xhb, wa_half) + ba_half)
    t_i = jnp.tanh(_mm(xhb, wx_half) + bx_half)
    m = t_r * half_decay + half_decay
    a = jnp.exp(-m)
    u = jnp.tanh(m) * (1.0 + a * a)
    mult = jnp.where(u > 0.0, u * lax.rsqrt(u), 0.0)
    hx = 0.5 * xh
    return a, mult * (t_i * hx + hx)


def _ffn_ple(layer, final, x_ref, p_ref, conv_fn, nf_ref, wg_ref, wu_ref, cw_ref, cb_ref, w_down_ref,
             np_ref, w_gate_ref, w_proj_ref, nfin_ref, o_ref, acc_s):
    rows = x_ref.shape[0]
    d_ff = w_down_ref.shape[0]
    cn = min(_FFN_CHUNK, d_ff)
    rbk = min(_PLE_ROWS, rows)
    blocks = [slice(r * rbk, (r + 1) * rbk) for r in range(rows // rbk)]
    xn_blocks = [_rmsnorm(x_ref[rs, :], _vec(nf_ref, layer)).astype(_BF16) for rs in blocks]
    xn = jnp.concatenate(xn_blocks, axis=0)
    for j in range(d_ff // cn):
        cs = slice(j * cn, (j + 1) * cn)
        if j == 0:
            g = jnp.concatenate([_mm(xb, wg_ref[:, cs]) for xb in xn_blocks], axis=0)
            u = jnp.concatenate([_mm(xb, wu_ref[:, cs]) for xb in xn_blocks], axis=0)
        else:
            g = _mm(xn, wg_ref[:, cs])
            u = _mm(xn, wu_ref[:, cs])
        gc = conv_fn(cs, g, cw_ref[:, cs], cb_ref[layer:layer + 1, cs])
        d = _mm((_gelu(gc) * u).astype(_BF16), w_down_ref[cs, :])
        if j == 0:
            acc_s[...] = d
        else:
            acc_s[...] += d
    for rs in blocks:
        h = x_ref[rs, :] + acc_s[rs, :]
        gate = _sigmoid(_mm(_rmsnorm(h, _vec(np_ref, layer)).astype(_BF16), w_gate_ref[...]))
        h = h + gate * _mm(p_ref[rs, :].astype(_BF16), w_proj_ref[...])
        o_ref[rs, :] = _rmsnorm(h, nfin_ref[...]) if final else h


def _lru_kernel(layer, j, cast, x_ref, h0_ref, past_ref, g_ref, w_in_ref, cw_ref, cb_ref, wa_ref, ba_ref,
                wx_ref, bx_ref, lam_ref, w_out_ref, o_ref, hlast_ref, tail_ref, *rest):
    if cast:
        w_in_bf, wa_bf, wx_bf, w_out_bf, tail_s, xb_s, gate_s, a_s, b_s, h_s = rest
        w_in_bf[...] = w_in_ref[...].astype(_BF16)
        wa_bf[...] = (0.5 * wa_ref[...]).astype(_BF16)
        wx_bf[...] = (0.5 * wx_ref[...]).astype(_BF16)
        w_out_bf[...] = w_out_ref[...].astype(_BF16)
        w_in_ref, wa_ref, wx_ref, w_out_ref = w_in_bf, wa_bf, wx_bf, w_out_bf
    else:
        tail_s, xb_s, gate_s, a_s, b_s, h_s = rest
    @pl.when(pl.program_id(0) == 0)
    def _():
        hlast_ref[...] = h0_ref[...]
        tail_s[...] = _to_tm(past_ref[...])

    heads, dh, _ = wa_ref.shape
    width = heads * dh
    bsz, tt, _ = x_ref.shape
    n_blk = _LRU_BLOCKS if tt % (_LRU_BLOCKS * _SUBLANES) == 0 else 1
    tb = tt // n_blk
    for r in range(n_blk):
        rs = slice(r * tb * bsz, (r + 1) * tb * bsz)
        x = _to_tm(x_ref[:, r * tb:(r + 1) * tb, :])
        xn = _rmsnorm(x, _vec(g_ref, layer)).astype(_BF16)
        xb_s[rs, :] = _mm(xn, w_in_ref[:, width:])
        gate_s[rs, :] = _mm(xn, w_in_ref[:, :width])
    half_decay = (0.5 * _LRU_C) * _softplus(-_vec(lam_ref, j))
    tail = tail_s[...]
    h = hlast_ref[...]
    for r in range(n_blk):
        r0 = r * tb * bsz
        rs = slice(r0, r0 + tb * bsz)
        xc, tail = _conv_tm(xb_s[rs, :], tail, cw_ref[...], _vec(cb_ref, j))
        for hh in range(heads):
            sl = slice(hh * dh, (hh + 1) * dh)
            a, b = _lru_coeffs(xc[:, sl], wa_ref[hh], ba_ref[j:j + 1, sl], wx_ref[hh], bx_ref[j:j + 1, sl],
                               half_decay[:, sl])
            a_s[rs, sl] = a
            b_s[rs, sl] = b
        for t in range(tb):
            ts = slice(r0 + t * bsz, r0 + (t + 1) * bsz)
            h = a_s[ts, :] * h + b_s[ts, :]
            h_s[ts, :] = h
        y = _mm((_gelu(gate_s[rs, :]) * h_s[rs, :]).astype(_BF16), w_out_ref[...])
        if len(o_ref.shape) == 3:
            tsl = slice(r * tb, (r + 1) * tb)
            o_ref[:, tsl, :] = x_ref[:, tsl, :] + _from_tm(y, tb)
        else:
            o_ref[rs, :] = _to_tm(x_ref[:, r * tb:(r + 1) * tb, :]) + y
    tail_s[...] = tail
    tail_ref[...] = _from_tm(tail, tail_ref.shape[1])
    hlast_ref[...] = h


def _gmlp_seq_kernel(layer, j, x_ref, g_ref, w_in_ref, lng_ref, lnb_ref, ws_ref, bs_ref, w_out_ref,
                     o_ref, s_s):
    rows = x_ref.shape[0]
    groups, chunk, _ = ws_ref.shape
    width = s_s.shape[1]
    gd = width // groups
    tri = (lax.broadcasted_iota(jnp.int32, (chunk, chunk), 0)
           >= lax.broadcasted_iota(jnp.int32, (chunk, chunk), 1))
    n_blk = _GMLP_BLOCKS if rows % (_GMLP_BLOCKS * chunk) == 0 else 1
    hr = rows // n_blk
    proj = []
    for h in range(n_blk):
        xn = _rmsnorm(x_ref[h * hr:(h + 1) * hr, :], _vec(g_ref, layer)).astype(_BF16)
        proj.append((_mm(xn, w_in_ref[:, width:]), _mm(xn, w_in_ref[:, :width])))
    n_c = hr // chunk
    for h in range(n_blk):
        zv, zu = proj[h]
        v = _layernorm(_gelu(zv), _vec(lng_ref, j), _vec(lnb_ref, j)).astype(_BF16)
        for g in range(groups):
            wg = jnp.where(tri, ws_ref[g], 0.0).astype(_BF16)
            cs = slice(g * gd, (g + 1) * gd)
            vg = jnp.concatenate([v[c * chunk:(c + 1) * chunk, cs] for c in range(n_c)], axis=1)
            sg = _mm(wg, vg)
            for c in range(n_c):
                s_s[h * hr + c * chunk:h * hr + (c + 1) * chunk, cs] = sg[:, c * gd:(c + 1) * gd] + bs_ref[:, cs]
        rs = slice(h * hr, (h + 1) * hr)
        o_ref[rs, :] = x_ref[rs, :] + _mm((_gelu(zu) * s_s[rs, :]).astype(_BF16), w_out_ref[...])


def _ffn_seq_kernel(layer, final, x_ref, p_ref, nf_ref, wg_ref, wu_ref, cw_ref, cb_ref, w_down_ref,
                    np_ref, w_gate_ref, w_proj_ref, nfin_ref, o_ref, tail_ref, prev_s, acc_s):
    @pl.when(pl.program_id(1) == 0)
    def _():
        prev_s[...] = jnp.zeros_like(prev_s)

    rows = x_ref.shape[0]
    keep = tail_ref.shape[0]

    def conv(cs, g, w, b):
        gc = _conv_rows(g, prev_s[:, cs], w, b)
        prev_s[:, cs] = g[rows - _SUBLANES:rows, :]
        tail_ref[:, cs] = g[rows - keep:rows, :]
        return gc

    _ffn_ple(layer, final, x_ref, p_ref, conv, nf_ref, wg_ref, wu_ref, cw_ref, cb_ref, w_down_ref,
             np_ref, w_gate_ref, w_proj_ref, nfin_ref, o_ref, acc_s)


def _gmlp_tm_kernel(layer, j, n_t, x_ref, g_ref, w_in_ref, lng_ref, lnb_ref, wsc_ref, bsc_ref, w_out_ref,
                    o_ref, v_ref, w_in_bf, w_out_bf, s_s):
    w_in_bf[...] = w_in_ref[...].astype(_BF16)
    w_out_bf[...] = w_out_ref[...].astype(_BF16)
    w_in_ref, w_out_ref = w_in_bf, w_out_bf
    x = x_ref[...]
    width = s_s.shape[1]
    bs = x.shape[0] // n_t
    z = _gelu(_mm(_rmsnorm(x, _vec(g_ref, layer)).astype(_BF16), w_in_ref[...]))
    u = z[:, :width]
    v = _layernorm(z[:, width:], _vec(lng_ref, j), _vec(lnb_ref, j))
    v_ref[...] = _from_tm(v, n_t)
    for t in range(n_t):
        s = bsc_ref[t:t + 1, :]
        for q in range(t + 1):
            s = s + wsc_ref[t, q:q + 1, :] * v[q * bs:(q + 1) * bs, :]
        s_s[t * bs:(t + 1) * bs, :] = s
    o_ref[...] = x + _mm((u * s_s[...]).astype(_BF16), w_out_ref[...])


def _ffn_tm_kernel(layer, final, chained, x_ref, p_ref, past_ref, nf_ref, wg_ref, wu_ref, cw_ref, cb_ref, wd_ref,
                   np_ref, w_gate_ref, w_proj_ref, nfin_ref, *rest):
    if chained:
        rest = rest[1:]
    o_ref, tail_ref, wg_bf_ref, wu_bf_ref, wd_bf_ref, gate_bf_ref, proj_bf_ref, xn_s, acc_s = rest
    j = pl.program_id(0)
    n_keep = past_ref.shape[1]

    @pl.when(j == 0)
    def _():
        xn_s[...] = _rmsnorm(x_ref[...], _vec(nf_ref, layer)).astype(_BF16)
        acc_s[...] = jnp.zeros_like(acc_s)

    wg = wg_ref[...].astype(_BF16)
    wu = wu_ref[...].astype(_BF16)
    wd = wd_ref[...].astype(_BF16)
    wg_bf_ref[...] = wg
    wu_bf_ref[...] = wu
    wd_bf_ref[...] = wd
    xn = xn_s[...]
    g = _mm(xn, wg)
    u = _mm(xn, wu)
    gc, tail = _conv_tm(g, _to_tm(past_ref[...]), cw_ref[...], cb_ref[layer:layer + 1, :])
    tail_ref[...] = _from_tm(tail, n_keep)
    acc_s[...] += _mm((_gelu(gc) * u).astype(_BF16), wd)

    @pl.when(j == pl.num_programs(0) - 1)
    def _():
        w_gate = w_gate_ref[...].astype(_BF16)
        w_proj = w_proj_ref[...].astype(_BF16)
        gate_bf_ref[...] = w_gate
        proj_bf_ref[...] = w_proj
        h = x_ref[...] + acc_s[...]
        gate = _sigmoid(_mm(_rmsnorm(h, _vec(np_ref, layer)).astype(_BF16), w_gate))
        h = h + gate * _mm(_to_tm(p_ref[...]).astype(_BF16), w_proj)
        if final:
            o_ref[...] = _from_tm(_rmsnorm(h, nfin_ref[...]), o_ref.shape[1])
        else:
            o_ref[...] = h


def _whole(a):
    zeros = (0,) * a.ndim
    return pl.BlockSpec(a.shape, lambda *_: zeros, pipeline_mode=pl.Buffered(1))


def _layer_of(a, i):
    zeros = (0,) * (a.ndim - 1)
    return pl.BlockSpec((None,) + a.shape[1:], lambda *_: (i,) + zeros, pipeline_mode=pl.Buffered(1))


def _params(n_axes):
    return pltpu.CompilerParams(dimension_semantics=("arbitrary",) * n_axes, vmem_limit_bytes=_VMEM_LIMIT)


def _row_spec(tm, d):
    return pl.BlockSpec((None, tm, d), lambda b, t: (b, t, 0))


def _state_spec(shape):
    zeros = (0,) * len(shape)
    return pl.BlockSpec((None,) + shape, lambda b, t: (b,) + zeros)


def _full_spec(shape):
    zeros = (0,) * len(shape)
    return pl.BlockSpec(shape, lambda *_: zeros)


def _lru_specs(w, j, prompt):
    if prompt:
        mats = [_whole(a) for a in w["lru_bf16"][j]]
    else:
        mats = [_layer_of(w[k], j) for k in ("lru_w_in", "lru_w_a", "lru_w_x", "lru_w_out")]
    return [_whole(w["norm_mix"]), mats[0], _layer_of(w["lru_conv_w"], j), _whole(w["lru_conv_b"]),
            mats[1], _whole(w["lru_b_a"]), mats[2], _whole(w["lru_b_x"]), _whole(w["lru_lambda"]), mats[3]]


def _lru_args(w, j, prompt):
    mats = w["lru_bf16"][j] if prompt else [w[k] for k in ("lru_w_in", "lru_w_a", "lru_w_x", "lru_w_out")]
    return [w["norm_mix"], mats[0], w["lru_conv_w"], w["lru_conv_b"], mats[1], w["lru_b_a"], mats[2],
            w["lru_b_x"], w["lru_lambda"], mats[3]]


def _ffn_specs(w, i):
    wg, wu, wd, w_gate, w_proj = w["ffn_bf16"][i]
    return [_whole(w["norm_ffn"]), _whole(wg), _whole(wu), _layer_of(w["ffn_conv_w"], i),
            _whole(w["ffn_conv_b"]), _whole(wd), _whole(w["norm_ple"]), _whole(w_gate), _whole(w_proj),
            _whole(w["norm_final"])]


def _ffn_args(w, i):
    wg, wu, wd, w_gate, w_proj = w["ffn_bf16"][i]
    return [w["norm_ffn"], wg, wu, w["ffn_conv_w"], w["ffn_conv_b"], wd, w["norm_ple"], w_gate, w_proj,
            w["norm_final"]]


def _lru(x, h0, past, w, layer, j, prompt):
    bsz, t, d = x.shape
    width = w["lru_w_out"].shape[1]
    if prompt:
        tt = min(_LRU_TILE // bsz, t)
        assert tt % _SUBLANES == 0 and t % tt == 0 and bsz % _SUBLANES == 0
        o_shape, o_spec = x.shape, pl.BlockSpec((bsz, tt, d), lambda t: (0, t, 0))
        copies = []
    else:
        tt = t
        o_shape, o_spec = (t * bsz, d), _full_spec((t * bsz, d))
        copies = [jax.ShapeDtypeStruct(w[k].shape[1:], _BF16) for k in ("lru_w_in", "lru_w_a", "lru_w_x", "lru_w_out")]
    out_shape = [jax.ShapeDtypeStruct(o_shape, _F32), jax.ShapeDtypeStruct(h0.shape, _F32),
                 jax.ShapeDtypeStruct(past.shape, _F32)] + copies
    outs = pl.pallas_call(
        functools.partial(_lru_kernel, layer, j, not prompt),
        grid=(t // tt,),
        in_specs=[pl.BlockSpec((bsz, tt, d), lambda t: (0, t, 0)), _whole(h0), _whole(past)]
        + _lru_specs(w, j, prompt),
        out_specs=[o_spec, _full_spec(h0.shape), _full_spec(past.shape)] + [_full_spec(c.shape) for c in copies],
        out_shape=out_shape,
        scratch_shapes=[pltpu.VMEM((past.shape[1] * bsz, width), _F32)]
        + [pltpu.VMEM((bsz * tt, width), _F32)] * 5,
        compiler_params=_params(1),
        name="lru_seq" if prompt else "lru_tm",
    )(x, h0, past, *_lru_args(w, j, prompt))
    return outs[0], outs[1], outs[2], tuple(outs[3:])


def _gmlp_seq(x, w, layer, j):
    bsz, t, d = x.shape
    w_in, w_out = w["gm_bf16"][j]
    width = w_out.shape[0]
    tm = min(_GMLP_TILE, t)
    return pl.pallas_call(
        functools.partial(_gmlp_seq_kernel, layer, j),
        grid=(bsz, t // tm),
        in_specs=[_row_spec(tm, d), _whole(w["norm_mix"]), _whole(w_in), _whole(w["gm_ln_g"]),
                  _whole(w["gm_ln_b"]), _layer_of(w["gm_w_s"], j), _layer_of(w["gm_b_s_rows"], j),
                  _whole(w_out)],
        out_specs=_row_spec(tm, d),
        out_shape=jax.ShapeDtypeStruct((bsz, t, d), _F32),
        scratch_shapes=[pltpu.VMEM((tm, width), _F32)],
        compiler_params=_params(2),
        name="gmlp_seq",
    )(x, w["norm_mix"], w_in, w["gm_ln_g"], w["gm_ln_b"], w["gm_w_s"], w["gm_b_s_rows"], w_out)


def _ffn_seq(x, p_all, w, layer, final, n_keep):
    bsz, t, d = x.shape
    pdim = p_all.shape[-1]
    d_ff = w["ffn_conv_w"].shape[2]
    tm = min(_FFN_TILE, t)
    p_spec = pl.BlockSpec((None, None, tm, pdim), lambda b, t: (layer, b, t, 0))
    return pl.pallas_call(
        functools.partial(_ffn_seq_kernel, layer, final),
        grid=(bsz, t // tm),
        in_specs=[_row_spec(tm, d), p_spec] + _ffn_specs(w, layer),
        out_specs=[_row_spec(tm, d), _state_spec((n_keep, d_ff))],
        out_shape=[jax.ShapeDtypeStruct((bsz, t, d), _F32),
                   jax.ShapeDtypeStruct((bsz, n_keep, d_ff), _F32)],
        scratch_shapes=[pltpu.VMEM((_SUBLANES, d_ff), _F32), pltpu.VMEM((tm, d), _F32)],
        compiler_params=_params(2),
        name="ffn_seq",
    )(x, p_all, *_ffn_args(w, layer))


def _gmlp_tm(x2d, n_t, w, layer, j):
    rows, d = x2d.shape
    w_in, w_out = w["gm_w_in"], w["gm_w_out"]
    width = w_out.shape[1]
    out_shape = [jax.ShapeDtypeStruct((rows, d), _F32), jax.ShapeDtypeStruct((rows // n_t, n_t, width), _F32),
                 jax.ShapeDtypeStruct(w_in.shape[1:], _BF16), jax.ShapeDtypeStruct(w_out.shape[1:], _BF16)]
    outs = pl.pallas_call(
        functools.partial(_gmlp_tm_kernel, layer, j, n_t),
        grid=(1,),
        in_specs=[_full_spec(x2d.shape), _whole(w["norm_mix"]), _layer_of(w_in, j), _whole(w["gm_ln_g"]),
                  _whole(w["gm_ln_b"]), _layer_of(w["gm_w_s_head"], j), _layer_of(w["gm_b_s_head"], j),
                  _layer_of(w_out, j)],
        out_specs=[_full_spec(s.shape) for s in out_shape],
        out_shape=out_shape,
        scratch_shapes=[pltpu.VMEM((rows, width), _F32)],
        compiler_params=_params(1),
        name="gmlp_tm",
    )(x2d, w["norm_mix"], w_in, w["gm_ln_g"], w["gm_ln_b"], w["gm_w_s_head"], w["gm_b_s_head"], w_out)
    return outs[0], outs[1], tuple(outs[2:])


def _ffn_tm(x2d, p_all, past_all, w, layer, final, tails=None):
    rows, d = x2d.shape
    _, bsz, n_t, pdim = p_all.shape
    n_keep, d_ff = past_all.shape[2], past_all.shape[3]
    w_up, w_down, w_gate, w_proj = w["ffn_w_up"], w["ffn_w_down"], w["ple_w_gate"], w["ple_w_proj"]
    kw = w["ffn_conv_w"].shape[1]
    n_layers = w["ffn_conv_b"].shape[0]
    cn = min(_FFN_TM_CHUNK, d_ff)
    n_chunks = d_ff // cn
    assert d_ff % cn == 0
    o_shape = (bsz, n_t, d) if final else (rows, d)
    out_shape = [jax.ShapeDtypeStruct(o_shape, _F32), jax.ShapeDtypeStruct(past_all.shape, _F32),
                 jax.ShapeDtypeStruct((d, d_ff), _BF16), jax.ShapeDtypeStruct((d, d_ff), _BF16),
                 jax.ShapeDtypeStruct((d_ff, d), _BF16), jax.ShapeDtypeStruct(w_gate.shape[1:], _BF16),
                 jax.ShapeDtypeStruct(w_proj.shape[1:], _BF16)]
    in_specs = [
        _full_spec(x2d.shape), _layer_of(p_all, layer),
        pl.BlockSpec((None, bsz, n_keep, cn), lambda j: (layer, 0, 0, j)),
        _whole(w["norm_ffn"]),
        pl.BlockSpec((None, d, cn), lambda j: (layer, 0, j)),
        pl.BlockSpec((None, d, cn), lambda j: (layer, 0, n_chunks + j)),
        pl.BlockSpec((None, kw, cn), lambda j: (layer, 0, j)),
        pl.BlockSpec((n_layers, cn), lambda j: (0, j)),
        pl.BlockSpec((None, cn, d), lambda j: (layer, j, 0)),
        _whole(w["norm_ple"]), _layer_of(w_gate, layer), _layer_of(w_proj, layer), _whole(w["norm_final"]),
    ]
    out_specs = [
        _full_spec(o_shape), pl.BlockSpec((None, bsz, n_keep, cn), lambda j: (layer, 0, 0, j)),
        pl.BlockSpec((d, cn), lambda j: (0, j)), pl.BlockSpec((d, cn), lambda j: (0, j)),
        pl.BlockSpec((cn, d), lambda j: (j, 0)), _full_spec(w_gate.shape[1:]), _full_spec(w_proj.shape[1:]),
    ]
    args = [x2d, p_all, past_all, w["norm_ffn"], w_up, w_up, w["ffn_conv_w"], w["ffn_conv_b"], w_down,
            w["norm_ple"], w_gate, w_proj, w["norm_final"]]
    aliases = {}
    if tails is not None:
        in_specs.append(pl.BlockSpec(memory_space=pl.ANY))
        aliases = {len(args): 1}
        args.append(tails)
    outs = pl.pallas_call(
        functools.partial(_ffn_tm_kernel, layer, final, tails is not None),
        grid=(n_chunks,),
        in_specs=in_specs,
        out_specs=out_specs,
        out_shape=out_shape,
        input_output_aliases=aliases,
        scratch_shapes=[pltpu.VMEM((rows, d), _BF16), pltpu.VMEM((rows, d), _F32)],
        compiler_params=_params(1),
        name="ffn_tm",
    )(*args)
    return outs[0], outs[1], tuple(outs[2:])


def kernel(x_prompt, x_sample, p_prompt, p_sample, state_lru_h, state_lru_conv, state_ffn_conv, norm_mix, norm_ffn, norm_ple, norm_final, lru_w_in, lru_conv_w, lru_conv_b, lru_w_a, lru_b_a, lru_w_x, lru_b_x, lru_lambda, lru_w_out, gm_w_in, gm_ln_g, gm_ln_b, gm_w_s, gm_b_s, gm_w_out, ffn_w_up, ffn_conv_w, ffn_conv_b, ffn_w_down, ple_w_gate, ple_w_proj):
    depth = norm_mix.shape[0]
    bsz, seq, d_model = x_prompt.shape
    dec_b, dec_t, _ = x_sample.shape
    d_ff = ffn_w_down.shape[1]
    groups, chunk = gm_w_s.shape[1], gm_w_s.shape[2]
    gd = gm_w_out.shape[1] // groups
    tm = min(_GMLP_TILE, seq)
    assert seq % tm == 0 and tm % chunk == 0 and tm % _SUBLANES == 0 and d_ff % min(_FFN_CHUNK, d_ff) == 0
    assert _PAST_LEN % chunk == 0 and dec_t <= chunk and dec_b % _SUBLANES == 0
    assert seq % min(_FFN_TILE, seq) == 0 and depth % 2 == 0

    w = dict(
        norm_mix=norm_mix, norm_ffn=norm_ffn, norm_ple=norm_ple, norm_final=norm_final.reshape(1, -1),
        lru_w_in=lru_w_in, lru_conv_w=lru_conv_w, lru_conv_b=lru_conv_b, lru_w_a=lru_w_a, lru_w_x=lru_w_x,
        lru_b_a=0.5 * lru_b_a, lru_b_x=0.5 * lru_b_x, lru_lambda=lru_lambda, lru_w_out=lru_w_out,
        gm_w_in=gm_w_in, gm_ln_g=gm_ln_g, gm_ln_b=gm_ln_b, gm_w_s=gm_w_s, gm_w_out=gm_w_out,
        gm_b_s_rows=jnp.repeat(jnp.transpose(gm_b_s, (0, 2, 1)), gd, axis=2),
        gm_w_s_head=jnp.repeat(jnp.transpose(gm_w_s[:, :, :dec_t, :dec_t], (0, 2, 3, 1)), gd, axis=3),
        gm_b_s_head=jnp.repeat(jnp.transpose(gm_b_s[:, :, :dec_t], (0, 2, 1)), gd, axis=2),
        ffn_w_up=ffn_w_up, ffn_conv_w=ffn_conv_w, ffn_conv_b=ffn_conv_b, ffn_w_down=ffn_w_down,
        ple_w_gate=ple_w_gate, ple_w_proj=ple_w_proj, lru_bf16={}, gm_bf16={}, ffn_bf16={},
    )

    hp = x_prompt
    hs = x_sample
    lru_h_p, lru_conv_p, ffn_conv_p = [], [], []
    lru_h_s, lru_conv_s, ffn_conv_s, gm_v_s = [], [], None, []
    for i in range(depth):
        j = i // 2
        if i % 2 == 0:
            kw = lru_conv_w.shape[1]
            width = lru_w_out.shape[1]
            if hs.ndim == 2:
                hs = _from_tm(hs, dec_t)
            hs, hlast, tail, w["lru_bf16"][j] = _lru(hs, state_lru_h[j], state_lru_conv[j], w, i, j, False)
            lru_h_s.append(hlast)
            lru_conv_s.append(tail)
            hp, hlast, tail, _ = _lru(hp, jnp.zeros((bsz, width), _F32), jnp.zeros((bsz, kw - 1, width), _F32),
                                      w, i, j, True)
            lru_h_p.append(hlast)
            lru_conv_p.append(tail)
        else:
            hs, v, w["gm_bf16"][j] = _gmlp_tm(hs, dec_t, w, i, j)
            gm_v_s.append(v)
            hp = _gmlp_seq(hp, w, i, j)
        kw = ffn_conv_w.shape[1]
        final = i == depth - 1
        hs, ffn_conv_s, w["ffn_bf16"][i] = _ffn_tm(hs, p_sample, state_ffn_conv, w, i, final,
                                                   ffn_conv_s if i else None)
        hp, tail = _ffn_seq(hp, p_prompt, w, i, final, kw - 1)
        ffn_conv_p.append(tail)

    def stack(xs):
        return xs[0][None] if len(xs) == 1 else jnp.stack(xs)

    return (hp, hs, stack(lru_h_p), stack(lru_conv_p), stack(ffn_conv_p),
            stack(lru_h_s), stack(lru_conv_s), ffn_conv_s, stack(gm_v_s))
```

```python
import functools
import math

import jax
import jax.numpy as jnp
from jax import lax
from jax.experimental import pallas as pl
from jax.experimental.pallas import tpu as pltpu

_EPS = 1e-6
_LRU_C = 8.0
_PAST_LEN = 16384
_SUBLANES = 8
_LRU_TILE = 1024
_GMLP_TILE = 1024
_FFN_TILE = 1024
_FFN_CHUNK = 1024
_GMLP_BLOCKS = 4
_PLE_ROWS = 256
_LRU_BLOCKS = 4
_FFN_TM_CHUNK = 512
_VMEM_LIMIT = 56 * 1024 * 1024

_F32 = jnp.float32
_BF16 = jnp.bfloat16


def _mm(a, b):
    return jnp.dot(a, b, preferred_element_type=_F32)


def _vec(ref, i):
    return ref[i:i + 1, :]


def _to_tm(a):
    b, t, c = a.shape
    return jnp.swapaxes(a, 0, 1).reshape(t * b, c)


def _from_tm(a2d, n_t):
    rows, c = a2d.shape
    return jnp.swapaxes(a2d.reshape(n_t, rows // n_t, c), 0, 1)


def _rmsnorm(x, g):
    ms = jnp.mean(x * x, axis=-1, keepdims=True)
    return (x * lax.rsqrt(ms + _EPS)) * g


def _layernorm(x, g, b):
    mu = jnp.mean(x, axis=-1, keepdims=True)
    xc = x - mu
    y = xc * lax.rsqrt(jnp.mean(xc * xc, axis=-1, keepdims=True) + _EPS)
    return y * g + b


def _gelu(x):
    c = math.sqrt(2.0 / math.pi)
    return x * (0.5 * (1.0 + jnp.tanh(c * (x + 0.044715 * (x * x * x)))))


def _softplus(x):
    return jnp.maximum(x, 0.0) + jnp.log1p(jnp.exp(-jnp.abs(x)))


def _conv_rows(x, prev8, w, b):
    kw = w.shape[0]
    x0 = x[0:_SUBLANES]
    row = lax.broadcasted_iota(jnp.int32, x0.shape, 0)
    y = None
    y0 = None
    for k in range(kw):
        d = kw - 1 - k
        wk = w[k:k + 1]
        if d == 0:
            t, t0 = x * wk, x0 * wk
        else:
            t = pltpu.roll(x, d, 0) * wk
            t0 = jnp.where(row < d, pltpu.roll(prev8, d, 0), pltpu.roll(x0, d, 0)) * wk
        y = t if y is None else y + t
        y0 = t0 if y0 is None else y0 + t0
    return jnp.concatenate([y0, y[_SUBLANES:]], axis=0) + b


def _conv_tm(x, past, w, b):
    kw = w.shape[0]
    rows = x.shape[0]
    rb = past.shape[0] // (kw - 1)
    ext = jnp.concatenate([past, x], axis=0)
    y = ext[0:rows] * w[0:1]
    for k in range(1, kw):
        y = y + ext[k * rb:k * rb + rows] * w[k:k + 1]
    return y + b, ext[rows:]


def _sigmoid(z):
    return 0.5 * jnp.tanh(0.5 * z) + 0.5


def _lru_coeffs(xh, wa_half, ba_half, wx_half, bx_half, half_decay):
    xhb = xh.astype(_BF16)
    t_r = jnp.tanh(_mm(xhb, wa_half) + ba_half)
    t_i = jnp.tanh(_mm(xhb, wx_half) + bx_half)
    m = t_r * half_decay + half_decay
    a = jnp.exp(-m)
    u = jnp.tanh(m) * (1.0 + a * a)
    mult = jnp.where(u > 0.0, u * lax.rsqrt(u), 0.0)
    hx = 0.5 * xh
    return a, mult * (t_i * hx + hx)


def _ffn_ple(layer, final, x_ref, p_ref, conv_fn, nf_ref, wg_ref, wu_ref, cw_ref, cb_ref, w_down_ref,
             np_ref, w_gate_ref, w_proj_ref, nfin_ref, o_ref, acc_s):
    rows = x_ref.shape[0]
    d_ff = w_down_ref.shape[0]
    cn = min(_FFN_CHUNK, d_ff)
    rbk = min(_PLE_ROWS, rows)
    blocks = [slice(r * rbk, (r + 1) * rbk) for r in range(rows // rbk)]
    for rs in blocks:
        o_ref[rs, :] = _mm(p_ref[rs, :].astype(_BF16), w_proj_ref[...])
    xn_blocks = [_rmsnorm(x_ref[rs, :], _vec(nf_ref, layer)).astype(_BF16) for rs in blocks]
    xn = jnp.concatenate(xn_blocks, axis=0)
    for j in range(d_ff // cn):
        cs = slice(j * cn, (j + 1) * cn)
        if j == 0:
            g = jnp.concatenate([_mm(xb, wg_ref[:, cs]) for xb in xn_blocks], axis=0)
            u = jnp.concatenate([_mm(xb, wu_ref[:, cs]) for xb in xn_blocks], axis=0)
        else:
            g = _mm(xn, wg_ref[:, cs])
            u = _mm(xn, wu_ref[:, cs])
        gc = conv_fn(cs, g, cw_ref[:, cs], cb_ref[layer:layer + 1, cs])
        d = _mm((_gelu(gc) * u).astype(_BF16), w_down_ref[cs, :])
        if j == 0:
            acc_s[...] = d
        else:
            acc_s[...] += d
    for rs in blocks:
        h = x_ref[rs, :] + acc_s[rs, :]
        gate = _sigmoid(_mm(_rmsnorm(h, _vec(np_ref, layer)).astype(_BF16), w_gate_ref[...]))
        h = h + gate * o_ref[rs, :]
        o_ref[rs, :] = _rmsnorm(h, nfin_ref[...]) if final else h


def _lru_kernel(layer, j, cast, x_ref, h0_ref, past_ref, g_ref, w_in_ref, cw_ref, cb_ref, wa_ref, ba_ref,
                wx_ref, bx_ref, lam_ref, w_out_ref, o_ref, hlast_ref, tail_ref, *rest):
    if cast:
        w_in_bf, wa_bf, wx_bf, w_out_bf, tail_s, xb_s, gate_s, a_s, b_s, h_s = rest
        w_in_bf[...] = w_in_ref[...].astype(_BF16)
        wa_bf[...] = (0.5 * wa_ref[...]).astype(_BF16)
        wx_bf[...] = (0.5 * wx_ref[...]).astype(_BF16)
        w_out_bf[...] = w_out_ref[...].astype(_BF16)
        w_in_ref, wa_ref, wx_ref, w_out_ref = w_in_bf, wa_bf, wx_bf, w_out_bf
    else:
        tail_s, xb_s, gate_s, a_s, b_s, h_s = rest
    @pl.when(pl.program_id(0) == 0)
    def _():
        hlast_ref[...] = h0_ref[...]
        tail_s[...] = _to_tm(past_ref[...])

    heads, dh, _ = wa_ref.shape
    width = heads * dh
    bsz, tt, _ = x_ref.shape
    n_blk = _LRU_BLOCKS if tt % (_LRU_BLOCKS * _SUBLANES) == 0 else 1
    tb = tt // n_blk
    for r in range(n_blk):
        rs = slice(r * tb * bsz, (r + 1) * tb * bsz)
        x = _to_tm(x_ref[:, r * tb:(r + 1) * tb, :])
        xn = _rmsnorm(x, _vec(g_ref, layer)).astype(_BF16)
        xb_s[rs, :] = _mm(xn, w_in_ref[:, width:])
        gate_s[rs, :] = _mm(xn, w_in_ref[:, :width])
    half_decay = (0.5 * _LRU_C) * _softplus(-_vec(lam_ref, j))
    tail = tail_s[...]
    h = hlast_ref[...]
    for r in range(n_blk):
        r0 = r * tb * bsz
        rs = slice(r0, r0 + tb * bsz)
        xc, tail = _conv_tm(xb_s[rs, :], tail, cw_ref[...], _vec(cb_ref, j))
        for hh in range(heads):
            sl = slice(hh * dh, (hh + 1) * dh)
            a, b = _lru_coeffs(xc[:, sl], wa_ref[hh], ba_ref[j:j + 1, sl], wx_ref[hh], bx_ref[j:j + 1, sl],
                               half_decay[:, sl])
            a_s[rs, sl] = a
            b_s[rs, sl] = b
        for t in range(tb):
            ts = slice(r0 + t * bsz, r0 + (t + 1) * bsz)
            h = a_s[ts, :] * h + b_s[ts, :]
            h_s[ts, :] = h
        y = _mm((_gelu(gate_s[rs, :]) * h_s[rs, :]).astype(_BF16), w_out_ref[...])
        if len(o_ref.shape) == 3:
            tsl = slice(r * tb, (r + 1) * tb)
            o_ref[:, tsl, :] = x_ref[:, tsl, :] + _from_tm(y, tb)
        else:
            o_ref[rs, :] = _to_tm(x_ref[:, r * tb:(r + 1) * tb, :]) + y
    tail_s[...] = tail
    tail_ref[...] = _from_tm(tail, tail_ref.shape[1])
    hlast_ref[...] = h


def _gmlp_seq_kernel(layer, j, x_ref, g_ref, w_in_ref, lng_ref, lnb_ref, ws_ref, bs_ref, w_out_ref,
                     o_ref, s_s):
    rows = x_ref.shape[0]
    groups, chunk, _ = ws_ref.shape
    width = s_s.shape[1]
    gd = width // groups
    tri = (lax.broadcasted_iota(jnp.int32, (chunk, chunk), 0)
           >= lax.broadcasted_iota(jnp.int32, (chunk, chunk), 1))
    n_blk = _GMLP_BLOCKS if rows % (_GMLP_BLOCKS * chunk) == 0 else 1
    hr = rows // n_blk
    proj = []
    for h in range(n_blk):
        xn = _rmsnorm(x_ref[h * hr:(h + 1) * hr, :], _vec(g_ref, layer)).astype(_BF16)
        proj.append((_mm(xn, w_in_ref[:, width:]), _mm(xn, w_in_ref[:, :width])))
    n_c = hr // chunk
    for h in range(n_blk):
        zv, zu = proj[h]
        v = _layernorm(_gelu(zv), _vec(lng_ref, j), _vec(lnb_ref, j)).astype(_BF16)
        for g in range(groups):
            wg = jnp.where(tri, ws_ref[g], 0.0).astype(_BF16)
            cs = slice(g * gd, (g + 1) * gd)
            vg = jnp.concatenate([v[c * chunk:(c + 1) * chunk, cs] for c in range(n_c)], axis=1)
            sg = _mm(wg, vg)
            for c in range(n_c):
                s_s[h * hr + c * chunk:h * hr + (c + 1) * chunk, cs] = sg[:, c * gd:(c + 1) * gd] + bs_ref[:, cs]
        rs = slice(h * hr, (h + 1) * hr)
        o_ref[rs, :] = x_ref[rs, :] + _mm((_gelu(zu) * s_s[rs, :]).astype(_BF16), w_out_ref[...])


def _ffn_seq_kernel(layer, final, x_ref, p_ref, nf_ref, wg_ref, wu_ref, cw_ref, cb_ref, w_down_ref,
                    np_ref, w_gate_ref, w_proj_ref, nfin_ref, o_ref, tail_ref, prev_s, acc_s):
    @pl.when(pl.program_id(1) == 0)
    def _():
        prev_s[...] = jnp.zeros_like(prev_s)

    rows = x_ref.shape[0]
    keep = tail_ref.shape[0]

    def conv(cs, g, w, b):
        gc = _conv_rows(g, prev_s[:, cs], w, b)
        prev_s[:, cs] = g[rows - _SUBLANES:rows, :]
        tail_ref[:, cs] = g[rows - keep:rows, :]
        return gc

    _ffn_ple(layer, final, x_ref, p_ref, conv, nf_ref, wg_ref, wu_ref, cw_ref, cb_ref, w_down_ref,
             np_ref, w_gate_ref, w_proj_ref, nfin_ref, o_ref, acc_s)


def _gmlp_tm_kernel(layer, j, n_t, x_ref, g_ref, w_in_ref, lng_ref, lnb_ref, wsc_ref, bsc_ref, w_out_ref,
                    o_ref, v_ref, w_in_bf, w_out_bf, s_s):
    w_in_bf[...] = w_in_ref[...].astype(_BF16)
    w_out_bf[...] = w_out_ref[...].astype(_BF16)
    w_in_ref, w_out_ref = w_in_bf, w_out_bf
    x = x_ref[...]
    width = s_s.shape[1]
    bs = x.shape[0] // n_t
    z = _gelu(_mm(_rmsnorm(x, _vec(g_ref, layer)).astype(_BF16), w_in_ref[...]))
    u = z[:, :width]
    v = _layernorm(z[:, width:], _vec(lng_ref, j), _vec(lnb_ref, j))
    v_ref[...] = _from_tm(v, n_t)
    for t in range(n_t):
        s = bsc_ref[t:t + 1, :]
        for q in range(t + 1):
            s = s + wsc_ref[t, q:q + 1, :] * v[q * bs:(q + 1) * bs, :]
        s_s[t * bs:(t + 1) * bs, :] = s
    o_ref[...] = x + _mm((u * s_s[...]).astype(_BF16), w_out_ref[...])


def _ffn_tm_kernel(layer, final, x_ref, p_ref, past_ref, nf_ref, wg_ref, wu_ref, cw_ref, cb_ref, wd_ref,
                   np_ref, w_gate_ref, w_proj_ref, nfin_ref, o_ref, tail_ref, wg_bf_ref, wu_bf_ref,
                   wd_bf_ref, gate_bf_ref, proj_bf_ref, xn_s, acc_s):
    j = pl.program_id(0)
    n_keep = past_ref.shape[1]

    @pl.when(j == 0)
    def _():
        xn_s[...] = _rmsnorm(x_ref[...], _vec(nf_ref, layer)).astype(_BF16)
        acc_s[...] = jnp.zeros_like(acc_s)

    wg = wg_ref[...].astype(_BF16)
    wu = wu_ref[...].astype(_BF16)
    wd = wd_ref[...].astype(_BF16)
    wg_bf_ref[...] = wg
    wu_bf_ref[...] = wu
    wd_bf_ref[...] = wd
    xn = xn_s[...]
    g = _mm(xn, wg)
    u = _mm(xn, wu)
    gc, tail = _conv_tm(g, _to_tm(past_ref[...]), cw_ref[...], cb_ref[layer:layer + 1, :])
    tail_ref[...] = _from_tm(tail, n_keep)
    acc_s[...] += _mm((_gelu(gc) * u).astype(_BF16), wd)

    @pl.when(j == pl.num_programs(0) - 1)
    def _():
        w_gate = w_gate_ref[...].astype(_BF16)
        w_proj = w_proj_ref[...].astype(_BF16)
        gate_bf_ref[...] = w_gate
        proj_bf_ref[...] = w_proj
        h = x_ref[...] + acc_s[...]
        gate = _sigmoid(_mm(_rmsnorm(h, _vec(np_ref, layer)).astype(_BF16), w_gate))
        h = h + gate * _mm(_to_tm(p_ref[...]).astype(_BF16), w_proj)
        if final:
            o_ref[...] = _from_tm(_rmsnorm(h, nfin_ref[...]), o_ref.shape[1])
        else:
            o_ref[...] = h


def _whole(a):
    zeros = (0,) * a.ndim
    return pl.BlockSpec(a.shape, lambda *_: zeros, pipeline_mode=pl.Buffered(1))


def _layer_of(a, i):
    zeros = (0,) * (a.ndim - 1)
    return pl.BlockSpec((None,) + a.shape[1:], lambda *_: (i,) + zeros, pipeline_mode=pl.Buffered(1))


def _params(n_axes):
    return pltpu.CompilerParams(dimension_semantics=("arbitrary",) * n_axes, vmem_limit_bytes=_VMEM_LIMIT)


def _row_spec(tm, d):
    return pl.BlockSpec((None, tm, d), lambda b, t: (b, t, 0))


def _state_spec(shape):
    zeros = (0,) * len(shape)
    return pl.BlockSpec((None,) + shape, lambda b, t: (b,) + zeros)


def _full_spec(shape):
    zeros = (0,) * len(shape)
    return pl.BlockSpec(shape, lambda *_: zeros)


def _lru_specs(w, j, prompt):
    if prompt:
        mats = [_whole(a) for a in w["lru_bf16"][j]]
    else:
        mats = [_layer_of(w[k], j) for k in ("lru_w_in", "lru_w_a", "lru_w_x", "lru_w_out")]
    return [_whole(w["norm_mix"]), mats[0], _layer_of(w["lru_conv_w"], j), _whole(w["lru_conv_b"]),
            mats[1], _whole(w["lru_b_a"]), mats[2], _whole(w["lru_b_x"]), _whole(w["lru_lambda"]), mats[3]]


def _lru_args(w, j, prompt):
    mats = w["lru_bf16"][j] if prompt else [w[k] for k in ("lru_w_in", "lru_w_a", "lru_w_x", "lru_w_out")]
    return [w["norm_mix"], mats[0], w["lru_conv_w"], w["lru_conv_b"], mats[1], w["lru_b_a"], mats[2],
            w["lru_b_x"], w["lru_lambda"], mats[3]]


def _ffn_specs(w, i):
    wg, wu, wd, w_gate, w_proj = w["ffn_bf16"][i]
    return [_whole(w["norm_ffn"]), _whole(wg), _whole(wu), _layer_of(w["ffn_conv_w"], i),
            _whole(w["ffn_conv_b"]), _whole(wd), _whole(w["norm_ple"]), _whole(w_gate), _whole(w_proj),
            _whole(w["norm_final"])]


def _ffn_args(w, i):
    wg, wu, wd, w_gate, w_proj = w["ffn_bf16"][i]
    return [w["norm_ffn"], wg, wu, w["ffn_conv_w"], w["ffn_conv_b"], wd, w["norm_ple"], w_gate, w_proj,
            w["norm_final"]]


def _lru(x, h0, past, w, layer, j, prompt):
    bsz, t, d = x.shape
    width = w["lru_w_out"].shape[1]
    if prompt:
        tt = min(_LRU_TILE // bsz, t)
        assert tt % _SUBLANES == 0 and t % tt == 0 and bsz % _SUBLANES == 0
        o_shape, o_spec = x.shape, pl.BlockSpec((bsz, tt, d), lambda t: (0, t, 0))
        copies = []
    else:
        tt = t
        o_shape, o_spec = (t * bsz, d), _full_spec((t * bsz, d))
        copies = [jax.ShapeDtypeStruct(w[k].shape[1:], _BF16) for k in ("lru_w_in", "lru_w_a", "lru_w_x", "lru_w_out")]
    out_shape = [jax.ShapeDtypeStruct(o_shape, _F32), jax.ShapeDtypeStruct(h0.shape, _F32),
                 jax.ShapeDtypeStruct(past.shape, _F32)] + copies
    outs = pl.pallas_call(
        functools.partial(_lru_kernel, layer, j, not prompt),
        grid=(t // tt,),
        in_specs=[pl.BlockSpec((bsz, tt, d), lambda t: (0, t, 0)), _whole(h0), _whole(past)]
        + _lru_specs(w, j, prompt),
        out_specs=[o_spec, _full_spec(h0.shape), _full_spec(past.shape)] + [_full_spec(c.shape) for c in copies],
        out_shape=out_shape,
        scratch_shapes=[pltpu.VMEM((past.shape[1] * bsz, width), _F32)]
        + [pltpu.VMEM((bsz * tt, width), _F32)] * 5,
        compiler_params=_params(1),
        name="lru_seq" if prompt else "lru_tm",
    )(x, h0, past, *_lru_args(w, j, prompt))
    return outs[0], outs[1], outs[2], tuple(outs[3:])


def _gmlp_seq(x, w, layer, j):
    bsz, t, d = x.shape
    w_in, w_out = w["gm_bf16"][j]
    width = w_out.shape[0]
    tm = min(_GMLP_TILE, t)
    return pl.pallas_call(
        functools.partial(_gmlp_seq_kernel, layer, j),
        grid=(bsz, t // tm),
        in_specs=[_row_spec(tm, d), _whole(w["norm_mix"]), _whole(w_in), _whole(w["gm_ln_g"]),
                  _whole(w["gm_ln_b"]), _layer_of(w["gm_w_s"], j), _layer_of(w["gm_b_s_rows"], j),
                  _whole(w_out)],
        out_specs=_row_spec(tm, d),
        out_shape=jax.ShapeDtypeStruct((bsz, t, d), _F32),
        scratch_shapes=[pltpu.VMEM((tm, width), _F32)],
        compiler_params=_params(2),
        name="gmlp_seq",
    )(x, w["norm_mix"], w_in, w["gm_ln_g"], w["gm_ln_b"], w["gm_w_s"], w["gm_b_s_rows"], w_out)


def _ffn_seq(x, p_all, w, layer, final, n_keep):
    bsz, t, d = x.shape
    pdim = p_all.shape[-1]
    d_ff = w["ffn_conv_w"].shape[2]
    tm = min(_FFN_TILE, t)
    p_spec = pl.BlockSpec((None, None, tm, pdim), lambda b, t: (layer, b, t, 0))
    return pl.pallas_call(
        functools.partial(_ffn_seq_kernel, layer, final),
        grid=(bsz, t // tm),
        in_specs=[_row_spec(tm, d), p_spec] + _ffn_specs(w, layer),
        out_specs=[_row_spec(tm, d), _state_spec((n_keep, d_ff))],
        out_shape=[jax.ShapeDtypeStruct((bsz, t, d), _F32),
                   jax.ShapeDtypeStruct((bsz, n_keep, d_ff), _F32)],
        scratch_shapes=[pltpu.VMEM((_SUBLANES, d_ff), _F32), pltpu.VMEM((tm, d), _F32)],
        compiler_params=_params(2),
        name="ffn_seq",
    )(x, p_all, *_ffn_args(w, layer))


def _gmlp_tm(x2d, n_t, w, layer, j):
    rows, d = x2d.shape
    w_in, w_out = w["gm_w_in"], w["gm_w_out"]
    width = w_out.shape[1]
    out_shape = [jax.ShapeDtypeStruct((rows, d), _F32), jax.ShapeDtypeStruct((rows // n_t, n_t, width), _F32),
                 jax.ShapeDtypeStruct(w_in.shape[1:], _BF16), jax.ShapeDtypeStruct(w_out.shape[1:], _BF16)]
    outs = pl.pallas_call(
        functools.partial(_gmlp_tm_kernel, layer, j, n_t),
        grid=(1,),
        in_specs=[_full_spec(x2d.shape), _whole(w["norm_mix"]), _layer_of(w_in, j), _whole(w["gm_ln_g"]),
                  _whole(w["gm_ln_b"]), _layer_of(w["gm_w_s_head"], j), _layer_of(w["gm_b_s_head"], j),
                  _layer_of(w_out, j)],
        out_specs=[_full_spec(s.shape) for s in out_shape],
        out_shape=out_shape,
        scratch_shapes=[pltpu.VMEM((rows, width), _F32)],
        compiler_params=_params(1),
        name="gmlp_tm",
    )(x2d, w["norm_mix"], w_in, w["gm_ln_g"], w["gm_ln_b"], w["gm_w_s_head"], w["gm_b_s_head"], w_out)
    return outs[0], outs[1], tuple(outs[2:])


def _ffn_tm(x2d, p_all, past_all, w, layer, final):
    rows, d = x2d.shape
    _, bsz, n_t, pdim = p_all.shape
    n_keep, d_ff = past_all.shape[2], past_all.shape[3]
    w_up, w_down, w_gate, w_proj = w["ffn_w_up"], w["ffn_w_down"], w["ple_w_gate"], w["ple_w_proj"]
    kw = w["ffn_conv_w"].shape[1]
    n_layers = w["ffn_conv_b"].shape[0]
    cn = min(_FFN_TM_CHUNK, d_ff)
    n_chunks = d_ff // cn
    assert d_ff % cn == 0
    o_shape = (bsz, n_t, d) if final else (rows, d)
    out_shape = [jax.ShapeDtypeStruct(o_shape, _F32), jax.ShapeDtypeStruct(past_all.shape[1:], _F32),
                 jax.ShapeDtypeStruct((d, d_ff), _BF16), jax.ShapeDtypeStruct((d, d_ff), _BF16),
                 jax.ShapeDtypeStruct((d_ff, d), _BF16), jax.ShapeDtypeStruct(w_gate.shape[1:], _BF16),
                 jax.ShapeDtypeStruct(w_proj.shape[1:], _BF16)]
    in_specs = [
        _full_spec(x2d.shape), _layer_of(p_all, layer),
        pl.BlockSpec((None, bsz, n_keep, cn), lambda j: (layer, 0, 0, j)),
        _whole(w["norm_ffn"]),
        pl.BlockSpec((None, d, cn), lambda j: (layer, 0, j)),
        pl.BlockSpec((None, d, cn), lambda j: (layer, 0, n_chunks + j)),
        pl.BlockSpec((None, kw, cn), lambda j: (layer, 0, j)),
        pl.BlockSpec((n_layers, cn), lambda j: (0, j)),
        pl.BlockSpec((None, cn, d), lambda j: (layer, j, 0)),
        _whole(w["norm_ple"]), _layer_of(w_gate, layer), _layer_of(w_proj, layer), _whole(w["norm_final"]),
    ]
    out_specs = [
        _full_spec(o_shape), pl.BlockSpec((bsz, n_keep, cn), lambda j: (0, 0, j)),
        pl.BlockSpec((d, cn), lambda j: (0, j)), pl.BlockSpec((d, cn), lambda j: (0, j)),
        pl.BlockSpec((cn, d), lambda j: (j, 0)), _full_spec(w_gate.shape[1:]), _full_spec(w_proj.shape[1:]),
    ]
    outs = pl.pallas_call(
        functools.partial(_ffn_tm_kernel, layer, final),
        grid=(n_chunks,),
        in_specs=in_specs,
        out_specs=out_specs,
        out_shape=out_shape,
        scratch_shapes=[pltpu.VMEM((rows, d), _BF16), pltpu.VMEM((rows, d), _F32)],
        compiler_params=_params(1),
        name="ffn_tm",
    )(x2d, p_all, past_all, w["norm_ffn"], w_up, w_up, w["ffn_conv_w"], w["ffn_conv_b"], w_down,
      w["norm_ple"], w_gate, w_proj, w["norm_final"])
    return outs[0], outs[1], tuple(outs[2:])


def kernel(x_prompt, x_sample, p_prompt, p_sample, state_lru_h, state_lru_conv, state_ffn_conv, norm_mix, norm_ffn, norm_ple, norm_final, lru_w_in, lru_conv_w, lru_conv_b, lru_w_a, lru_b_a, lru_w_x, lru_b_x, lru_lambda, lru_w_out, gm_w_in, gm_ln_g, gm_ln_b, gm_w_s, gm_b_s, gm_w_out, ffn_w_up, ffn_conv_w, ffn_conv_b, ffn_w_down, ple_w_gate, ple_w_proj):
    depth = norm_mix.shape[0]
    bsz, seq, d_model = x_prompt.shape
    dec_b, dec_t, _ = x_sample.shape
    d_ff = ffn_w_down.shape[1]
    groups, chunk = gm_w_s.shape[1], gm_w_s.shape[2]
    gd = gm_w_out.shape[1] // groups
    tm = min(_GMLP_TILE, seq)
    assert seq % tm == 0 and tm % chunk == 0 and tm % _SUBLANES == 0 and d_ff % min(_FFN_CHUNK, d_ff) == 0
    assert _PAST_LEN % chunk == 0 and dec_t <= chunk and dec_b % _SUBLANES == 0
    assert seq % min(_FFN_TILE, seq) == 0 and depth % 2 == 0

    w = dict(
        norm_mix=norm_mix, norm_ffn=norm_ffn, norm_ple=norm_ple, norm_final=norm_final.reshape(1, -1),
        lru_w_in=lru_w_in, lru_conv_w=lru_conv_w, lru_conv_b=lru_conv_b, lru_w_a=lru_w_a, lru_w_x=lru_w_x,
        lru_b_a=0.5 * lru_b_a, lru_b_x=0.5 * lru_b_x, lru_lambda=lru_lambda, lru_w_out=lru_w_out,
        gm_w_in=gm_w_in, gm_ln_g=gm_ln_g, gm_ln_b=gm_ln_b, gm_w_s=gm_w_s, gm_w_out=gm_w_out,
        gm_b_s_rows=jnp.repeat(jnp.transpose(gm_b_s, (0, 2, 1)), gd, axis=2),
        gm_w_s_head=jnp.repeat(jnp.transpose(gm_w_s[:, :, :dec_t, :dec_t], (0, 2, 3, 1)), gd, axis=3),
        gm_b_s_head=jnp.repeat(jnp.transpose(gm_b_s[:, :, :dec_t], (0, 2, 1)), gd, axis=2),
        ffn_w_up=ffn_w_up, ffn_conv_w=ffn_conv_w, ffn_conv_b=ffn_conv_b, ffn_w_down=ffn_w_down,
        ple_w_gate=ple_w_gate, ple_w_proj=ple_w_proj, lru_bf16={}, gm_bf16={}, ffn_bf16={},
    )

    hp = x_prompt
    hs = x_sample
    lru_h_p, lru_conv_p, ffn_conv_p = [], [], []
    lru_h_s, lru_conv_s, ffn_conv_s, gm_v_s = [], [], [], []
    for i in range(depth):
        j = i // 2
        if i % 2 == 0:
            kw = lru_conv_w.shape[1]
            width = lru_w_out.shape[1]
            if hs.ndim == 2:
                hs = _from_tm(hs, dec_t)
            hs, hlast, tail, w["lru_bf16"][j] = _lru(hs, state_lru_h[j], state_lru_conv[j], w, i, j, False)
            lru_h_s.append(hlast)
            lru_conv_s.append(tail)
            hp, hlast, tail, _ = _lru(hp, jnp.zeros((bsz, width), _F32), jnp.zeros((bsz, kw - 1, width), _F32),
                                      w, i, j, True)
            lru_h_p.append(hlast)
            lru_conv_p.append(tail)
        else:
            hs, v, w["gm_bf16"][j] = _gmlp_tm(hs, dec_t, w, i, j)
            gm_v_s.append(v)
            hp = _gmlp_seq(hp, w, i, j)
        kw = ffn_conv_w.shape[1]
        final = i == depth - 1
        hs, tail, w["ffn_bf16"][i] = _ffn_tm(hs, p_sample, state_ffn_conv, w, i, final)
        ffn_conv_s.append(tail)
        hp, tail = _ffn_seq(hp, p_prompt, w, i, final, kw - 1)
        ffn_conv_p.append(tail)

    def stack(xs):
        return xs[0][None] if len(xs) == 1 else jnp.stack(xs)

    return (hp, hs, stack(lru_h_p), stack(lru_conv_p), stack(ffn_conv_p),
            stack(lru_h_s), stack(lru_conv_s), stack(ffn_conv_s), stack(gm_v_s))
```
